```python
import math
import jax, jax.numpy as jnp
from jax import lax
import numpy as np

D_MODEL = 2048
BATCH = 2
SEQ = 16384
DEPTH = 1

CHUNK = 64
N_META = 16
SSM_WIDTH = D_MODEL // 2
SSM_GROUP = 16
SSM_GROUPS = SSM_WIDTH // SSM_GROUP
SSM_STATE = 64
SB_WIDTH = D_MODEL // 2
SB_HEADS = 8
SB_HEAD_DIM = SB_WIDTH // SB_HEADS
Q_BLOCK = 128
IN_WIDTH = 2 * SSM_WIDTH + 4 * SB_WIDTH
DEEPNORM_ALPHA = (2.0 * DEPTH) ** 0.25
DEEPNORM_BETA = (8.0 * DEPTH) ** -0.25
LN_EPS = 1e-5
DT_MIN = 1e-3
DT_MAX = 1e-1

kernel_name = "hybrid_s5_stickbreaking_gated_deepnorm"


def layer_norm(x, gain, bias):
    xf = x.astype(jnp.float32)
    mu = jnp.mean(xf, axis=-1, keepdims=True)
    var = jnp.mean(jnp.square(xf - mu), axis=-1, keepdims=True)
    return (xf - mu) * lax.rsqrt(var + LN_EPS) * gain.astype(jnp.float32) + bias.astype(jnp.float32)


def _cmul(ar, ai, br, bi):
    return ar * br - ai * bi, ar * bi + ai * br


def s5_branch(u, lam_re, lam_im, log_dt, b_re, b_im, c_re, c_im, d_skip, w_glu, b_glu):
    bsz, L, _ = u.shape
    uf = u.astype(jnp.float32).reshape(bsz, L, SSM_GROUPS, SSM_GROUP)
    lr = lam_re.astype(jnp.float32)
    li = lam_im.astype(jnp.float32)
    dt = jnp.exp(log_dt.astype(jnp.float32))[:, None]
    mag = jnp.exp(lr * dt)
    ab_r = mag * jnp.cos(li * dt)
    ab_i = mag * jnp.sin(li * dt)
    den = lr * lr + li * li
    nr = ab_r - 1.0
    f_r = (nr * lr + ab_i * li) / den
    f_i = (ab_i * lr - nr * li) / den
    br = b_re.astype(jnp.float32)
    bi = b_im.astype(jnp.float32)
    bb_r = f_r[..., None] * br - f_i[..., None] * bi
    bb_i = f_r[..., None] * bi + f_i[..., None] * br
    bu_r = jnp.einsum('gpc,blgc->blgp', bb_r, uf)
    bu_i = jnp.einsum('gpc,blgc->blgp', bb_i, uf)
    a_r = jnp.broadcast_to(ab_r[None, None], (1, L, SSM_GROUPS, SSM_STATE))
    a_i = jnp.broadcast_to(ab_i[None, None], (1, L, SSM_GROUPS, SSM_STATE))

    def combine(e1, e2):
        a1r, a1i, b1r, b1i = e1
        a2r, a2i, b2r, b2i = e2
        ar, ai = _cmul(a2r, a2i, a1r, a1i)
        sr, si = _cmul(a2r, a2i, b1r, b1i)
        return ar, ai, sr + b2r, si + b2i

    _, _, h_r, h_i = lax.associative_scan(combine, (a_r, a_i, bu_r, bu_i), axis=1)
    y = (jnp.einsum('gcp,blgp->blgc', c_re.astype(jnp.float32), h_r)
         - jnp.einsum('gcp,blgp->blgc', c_im.astype(jnp.float32), h_i)
         + d_skip.astype(jnp.float32) * uf)
    y = y.reshape(bsz, L, SSM_WIDTH)
    z = jax.nn.gelu(y)
    hg = z @ w_glu.astype(jnp.float32) + b_glu.astype(jnp.float32)
    return hg[..., :SSM_WIDTH] * jax.nn.sigmoid(hg[..., SSM_WIDTH:])


def stick_breaking_attention(q, k, v):
    bsz, nh, L, d = q.shape
    n_blk = -(-L // Q_BLOCK)
    L_pad = n_blk * Q_BLOCK
    q = jnp.pad(q, ((0, 0), (0, 0), (0, L_pad - L), (0, 0)))
    q_blocks = q.reshape(bsz, nh, n_blk, Q_BLOCK, d).transpose(2, 0, 1, 3, 4)
    starts = jnp.arange(n_blk, dtype=jnp.int32) * Q_BLOCK
    k_pos = jnp.arange(L, dtype=jnp.int32)
    scale = 1.0 / math.sqrt(d)

    def block(args):
        qb, start = args
        z = jnp.einsum('bhqd,bhkd->bhqk', qb, k) * scale
        q_pos = start + jnp.arange(Q_BLOCK, dtype=jnp.int32)
        mask = k_pos[None, :] < q_pos[:, None]
        log_keep = jnp.where(mask, jax.nn.log_sigmoid(-z), 0.0)
        suffix = lax.cumsum(log_keep, axis=3, reverse=True) - log_keep
        a = jnp.where(mask, jnp.exp(jax.nn.log_sigmoid(z) + suffix), 0.0)
        return jnp.einsum('bhqk,bhkd->bhqd', a, v)

    o = lax.map(block, (q_blocks, starts))
    o = o.transpose(1, 2, 0, 3, 4).reshape(bsz, nh, L_pad, d)
    return o[:, :, :L]


def setup_inputs(seed: int = 0) -> dict:
    key = jax.random.key(seed)
    ks = jax.random.split(key, 20)
    f32 = jnp.float32
    n = jnp.arange(SSM_STATE, dtype=f32)
    lam_re = -0.5 + 1e-3 * jax.random.normal(ks[2], (DEPTH, SSM_GROUPS, SSM_STATE), f32)
    lam_im = math.pi * n + 1e-3 * jax.random.normal(ks[3], (DEPTH, SSM_GROUPS, SSM_STATE), f32)
    log_dt = jax.random.uniform(ks[4], (DEPTH, SSM_GROUPS), f32,
                                minval=math.log(DT_MIN), maxval=math.log(DT_MAX))
    return {
        "x": jax.random.normal(ks[0], (BATCH, SEQ, D_MODEL), f32),
        "meta_tokens": jax.random.normal(ks[1], (N_META, D_MODEL), f32),
        "w_in": jax.random.normal(ks[5], (DEPTH, D_MODEL, IN_WIDTH), f32) * D_MODEL ** -0.5,
        "ssm_lambda_re": lam_re,
        "ssm_lambda_im": lam_im,
        "ssm_log_dt": log_dt,
        "ssm_b_re": jax.random.normal(ks[6], (DEPTH, SSM_GROUPS, SSM_STATE, SSM_GROUP), f32) * (2 * SSM_GROUP) ** -0.5,
        "ssm_b_im": jax.random.normal(ks[7], (DEPTH, SSM_GROUPS, SSM_STATE, SSM_GROUP), f32) * (2 * SSM_GROUP) ** -0.5,
        "ssm_c_re": jax.random.normal(ks[8], (DEPTH, SSM_GROUPS, SSM_GROUP, SSM_STATE), f32) * (2 * SSM_STATE) ** -0.5,
        "ssm_c_im": jax.random.normal(ks[9], (DEPTH, SSM_GROUPS, SSM_GROUP, SSM_STATE), f32) * (2 * SSM_STATE) ** -0.5,
        "ssm_d": jax.random.normal(ks[10], (DEPTH, SSM_GROUPS, SSM_GROUP), f32),
        "w_glu": jax.random.normal(ks[11], (DEPTH, SSM_WIDTH, 2 * SSM_WIDTH), f32) * SSM_WIDTH ** -0.5,
        "b_glu": 0.01 * jax.random.normal(ks[12], (DEPTH, 2 * SSM_WIDTH), f32),
        "w_branch_ssm": jax.random.normal(ks[13], (DEPTH, SSM_WIDTH, D_MODEL), f32) * SSM_WIDTH ** -0.5 * DEEPNORM_BETA,
        "w_branch_sb": jax.random.normal(ks[14], (DEPTH, SB_WIDTH, D_MODEL), f32) * SB_WIDTH ** -0.5 * DEEPNORM_BETA,
        "w_gate": jax.random.normal(ks[15], (DEPTH, D_MODEL, 2 * D_MODEL), f32) * D_MODEL ** -0.5,
        "b_gate": 0.01 * jax.random.normal(ks[16], (DEPTH, 2 * D_MODEL), f32),
        "w_out": jax.random.normal(ks[17], (DEPTH, D_MODEL, D_MODEL), f32) * D_MODEL ** -0.5 * DEEPNORM_BETA,
        "ln_gain": 1.0 + 0.02 * jax.random.normal(ks[18], (DEPTH, D_MODEL), f32),
        "ln_bias": 0.02 * jax.random.normal(ks[19], (DEPTH, D_MODEL), f32),
    }


def reference(x, meta_tokens, w_in, ssm_lambda_re, ssm_lambda_im, ssm_log_dt,
              ssm_b_re, ssm_b_im, ssm_c_re, ssm_c_im, ssm_d, w_glu, b_glu,
              w_branch_ssm, w_branch_sb, w_gate, b_gate, w_out, ln_gain, ln_bias):
    bsz = x.shape[0]
    meta = jnp.broadcast_to(meta_tokens[None].astype(x.dtype), (bsz, N_META, D_MODEL))
    h = jnp.concatenate([meta, x], axis=1)
    L = h.shape[1]
    split_at = [SSM_WIDTH, 2 * SSM_WIDTH, 2 * SSM_WIDTH + SB_WIDTH,
                2 * SSM_WIDTH + 2 * SB_WIDTH, 2 * SSM_WIDTH + 3 * SB_WIDTH]
    for layer in range(DEPTH):
        hf = h.astype(jnp.float32)
        proj = hf @ w_in[layer].astype(jnp.float32)
        u_ssm, g_ssm, q, k, v, g_sb = jnp.split(proj, split_at, axis=-1)

        y_ssm = s5_branch(u_ssm, ssm_lambda_re[layer], ssm_lambda_im[layer], ssm_log_dt[layer],
                          ssm_b_re[layer], ssm_b_im[layer], ssm_c_re[layer], ssm_c_im[layer],
                          ssm_d[layer].reshape(SSM_GROUPS, SSM_GROUP), w_glu[layer], b_glu[layer])
        y_ssm = (y_ssm * jax.nn.silu(g_ssm)) @ w_branch_ssm[layer].astype(jnp.float32)

        def heads(t):
            return t.reshape(bsz, L, SB_HEADS, SB_HEAD_DIM).transpose(0, 2, 1, 3)
        o = stick_breaking_attention(heads(q), heads(k), heads(v))
        y_sb = o.transpose(0, 2, 1, 3).reshape(bsz, L, SB_WIDTH)
        y_sb = (y_sb * jax.nn.silu(g_sb)) @ w_branch_sb[layer].astype(jnp.float32)

        gates = jax.nn.sigmoid(hf @ w_gate[layer].astype(jnp.float32) + b_gate[layer].astype(jnp.float32))
        mixed = gates[..., :D_MODEL] * y_ssm + gates[..., D_MODEL:] * y_sb
        sub = mixed @ w_out[layer].astype(jnp.float32)

        h = layer_norm(DEEPNORM_ALPHA * hf + sub, ln_gain[layer], ln_bias[layer]).astype(x.dtype)
    return h[:, N_META:]
```

```python
import functools
import math

import jax
import jax.numpy as jnp
from jax import lax
from jax.experimental import pallas as pl
from jax.experimental.pallas import tpu as pltpu

D_MODEL = 2048
N_META = 16
SSM_WIDTH = 1024
SSM_GROUP = 16
SSM_GROUPS = 64
SSM_STATE = 64
SB_WIDTH = 1024
SB_HEADS = 8
SB_HEAD_DIM = 128
IN_WIDTH = 2 * SSM_WIDTH + 4 * SB_WIDTH
DEEPNORM_ALPHA = 2.0 ** 0.25
LN_EPS = 1e-5

SUBLANES = 8
LANES = 128

FRONT = 1024
PROJ_TM = 1024
PROJ_TN = 1024
SSM_TM = 256
ATT_T = 256
MIX_TM = 512
MIX_TN = 512
META_ROWS = 256

SSM_CHUNKS = 4
CHUNK_CH = SSM_WIDTH // SSM_CHUNKS
CHUNK_ST = SSM_GROUPS * SSM_STATE // SSM_CHUNKS
N_STATE = SSM_GROUPS * SSM_STATE
SCAN_STEPS = (1, 2, 4)
N_POW = SUBLANES

EXP_ZERO_BELOW = -104.0

VMEM_LIMIT = 56 * 1024 * 1024


def _resident(block_shape, index_map):
    return pl.BlockSpec(block_shape, index_map, pipeline_mode=pl.Buffered(1))


def _ssm_params_kernel(lr_ref, li_ref, logdt_ref, br_ref, bi_ref,
                       bbr_ref, bbi_ref, pwr_ref, pwi_ref):
    lr = lr_ref[...]
    li = li_ref[...]
    dt = jnp.exp(logdt_ref[...])
    for e in range(1, N_POW + 1):
        mag = jnp.exp(lr * dt * float(e))
        ang = li * dt * float(e)
        pwr_ref[e - 1] = mag * jnp.cos(ang)
        pwi_ref[e - 1] = mag * jnp.sin(ang)
    ab_r = pwr_ref[0]
    ab_i = pwi_ref[0]
    den = lr * lr + li * li
    nr = ab_r - 1.0
    f_r = (nr * lr + ab_i * li) / den
    f_i = (ab_i * lr - nr * li) / den
    br = br_ref[...]
    bi = bi_ref[...]
    bbr_ref[...] = f_r[None] * br - f_i[None] * bi
    bbi_ref[...] = f_r[None] * bi + f_i[None] * br


def _ssm_params(lam_re, lam_im, log_dt, b_re, b_im):
    g, p, c = SSM_GROUPS, SSM_STATE, SSM_GROUP
    f32 = jnp.float32
    return pl.pallas_call(
        _ssm_params_kernel,
        out_shape=(jax.ShapeDtypeStruct((c, g, p), f32),
                   jax.ShapeDtypeStruct((c, g, p), f32),
                   jax.ShapeDtypeStruct((N_POW, g, p), f32),
                   jax.ShapeDtypeStruct((N_POW, g, p), f32)),
        name="ssm_params",
    )(lam_re.astype(f32), lam_im.astype(f32), log_dt.astype(f32).reshape(g, 1),
      jnp.transpose(b_re.astype(f32), (2, 0, 1)), jnp.transpose(b_im.astype(f32), (2, 0, 1)))


def _in_proj_kernel(x_ref, meta_ref, w_ref, o_ref, xb_ref):
    j = pl.program_id(1)
    n = pl.program_id(2)

    @pl.when(jnp.logical_and(n == 0, j == 0))
    def _():
        xb_ref[0:PROJ_TM - META_ROWS, :] = jnp.zeros((PROJ_TM - META_ROWS, D_MODEL), jnp.bfloat16)
        xb_ref[PROJ_TM - META_ROWS:PROJ_TM, :] = meta_ref[...].astype(jnp.bfloat16)

    @pl.when(jnp.logical_and(n == 0, j > 0))
    def _():
        xb_ref[...] = x_ref[0].astype(jnp.bfloat16)

    o_ref[0] = jnp.dot(xb_ref[...], w_ref[...],
                       preferred_element_type=jnp.float32).astype(o_ref.dtype)


def _in_proj(x, meta_block, w_in_bf):
    bsz, seq, _ = x.shape
    n_tiles = seq // PROJ_TM + FRONT // PROJ_TM
    lp = seq + FRONT
    return pl.pallas_call(
        _in_proj_kernel,
        grid=(bsz, n_tiles, IN_WIDTH // PROJ_TN),
        in_specs=[
            pl.BlockSpec((1, PROJ_TM, D_MODEL), lambda b, j, n: (b, jnp.maximum(j - 1, 0), 0)),
            _resident((META_ROWS, D_MODEL), lambda b, j, n: (0, 0)),
            pl.BlockSpec((D_MODEL, PROJ_TN), lambda b, j, n: (0, n)),
        ],
        out_specs=pl.BlockSpec((1, PROJ_TM, PROJ_TN), lambda b, j, n: (b, j, n)),
        out_shape=jax.ShapeDtypeStruct((bsz, lp, IN_WIDTH), jnp.bfloat16),
        scratch_shapes=[pltpu.VMEM((PROJ_TM, D_MODEL), jnp.bfloat16)],
        compiler_params=pltpu.CompilerParams(
            dimension_semantics=("arbitrary", "arbitrary", "arbitrary"),
            vmem_limit_bytes=VMEM_LIMIT),
        name="in_proj",
    )(x, meta_block, w_in_bf)


def _gelu_tanh(x):
    return 0.5 * x * (1.0 + jnp.tanh(math.sqrt(2.0 / math.pi) * (x + 0.044715 * (x * x * x))))


def _sigmoid(x):
    return 1.0 / (1.0 + jnp.exp(-x))


def _ssm_kernel(u_ref, g_ref, bw_ref, cw_ref, mr_ref, mi_ref, fr_ref, fi_ref,
                d_ref, wglu_ref, bglu_ref, o_ref, s_ref, carry_ref):
    i = pl.program_id(1)

    @pl.when(i == 0)
    def _():
        carry_ref[...] = jnp.zeros_like(carry_ref)

    u_bf = u_ref[0]
    for kc in range(SSM_CHUNKS):
        s_ref[:, kc * 2 * CHUNK_ST:(kc + 1) * 2 * CHUNK_ST] = jnp.dot(
            u_bf[:, kc * CHUNK_CH:(kc + 1) * CHUNK_CH], bw_ref[kc],
            preferred_element_type=jnp.float32)

    n_groups = SSM_TM // SUBLANES
    lane_tiles = CHUNK_ST // LANES
    for kc in range(SSM_CHUNKS):
        cols = [(kc * 2 * CHUNK_ST + c * LANES,
                 kc * 2 * CHUNK_ST + CHUNK_ST + c * LANES,
                 kc * CHUNK_ST + c * LANES) for c in range(lane_tiles)]
        init = []
        for (cr, ci, _) in cols:
            init.append(carry_ref[:, cr:cr + LANES])
            init.append(carry_ref[:, ci:ci + LANES])

        def body(m, carry, cols=cols):
            r0 = pl.multiple_of(m * SUBLANES, SUBLANES)
            new = []
            for idx, (cr, ci, cm) in enumerate(cols):
                xr = s_ref[pl.ds(r0, SUBLANES), cr:cr + LANES]
                xi = s_ref[pl.ds(r0, SUBLANES), ci:ci + LANES]
                for k, d in enumerate(SCAN_STEPS):
                    mr = mr_ref[k, :, cm:cm + LANES]
                    mi = mi_ref[k, :, cm:cm + LANES]
                    sr = pltpu.roll(xr, d, 0)
                    si = pltpu.roll(xi, d, 0)
                    xr, xi = xr + (mr * sr - mi * si), xi + (mr * si + mi * sr)
                fr = fr_ref[:, cm:cm + LANES]
                fi = fi_ref[:, cm:cm + LANES]
                pr = carry[2 * idx]
                pi = carry[2 * idx + 1]
                xr, xi = xr + (fr * pr - fi * pi), xi + (fr * pi + fi * pr)
                s_ref[pl.ds(r0, SUBLANES), cr:cr + LANES] = xr
                s_ref[pl.ds(r0, SUBLANES), ci:ci + LANES] = xi
                new.append(jnp.broadcast_to(xr[SUBLANES - 1:SUBLANES, :], (SUBLANES, LANES)))
                new.append(jnp.broadcast_to(xi[SUBLANES - 1:SUBLANES, :], (SUBLANES, LANES)))
            return tuple(new)

        final = lax.fori_loop(0, n_groups, body, tuple(init))
        for idx, (cr, ci, _) in enumerate(cols):
            carry_ref[:, cr:cr + LANES] = final[2 * idx]
            carry_ref[:, ci:ci + LANES] = final[2 * idx + 1]

    ys = []
    for kc in range(SSM_CHUNKS):
        h_bf = s_ref[:, kc * 2 * CHUNK_ST:(kc + 1) * 2 * CHUNK_ST].astype(jnp.bfloat16)
        ys.append(jnp.dot(h_bf, cw_ref[kc], preferred_element_type=jnp.float32))
    y = jnp.concatenate(ys, axis=1) + d_ref[...] * u_bf.astype(jnp.float32)
    z = _gelu_tanh(y).astype(jnp.bfloat16)
    hg = jnp.dot(z, wglu_ref[...], preferred_element_type=jnp.float32) + bglu_ref[...]
    glu = hg[:, :SSM_WIDTH] * _sigmoid(hg[:, SSM_WIDTH:])
    g = g_ref[0].astype(jnp.float32)
    o_ref[0] = (glu * (g * _sigmoid(g))).astype(o_ref.dtype)


def _ssm(proj, bw, cw, mr, mi, fr, fi, d_row, w_glu_bf, b_glu_row, seq):
    bsz = proj.shape[0]
    t0 = FRONT // SSM_TM - 1
    n_tiles = seq // SSM_TM + 1
    return pl.pallas_call(
        _ssm_kernel,
        grid=(bsz, n_tiles),
        in_specs=[
            pl.BlockSpec((1, SSM_TM, SSM_WIDTH), lambda b, i: (b, t0 + i, 0)),
            pl.BlockSpec((1, SSM_TM, SSM_WIDTH), lambda b, i: (b, t0 + i, 1)),
            _resident(bw.shape, lambda b, i: (0, 0, 0)),
            _resident(cw.shape, lambda b, i: (0, 0, 0)),
            _resident(mr.shape, lambda b, i: (0, 0, 0)),
            _resident(mi.shape, lambda b, i: (0, 0, 0)),
            _resident(fr.shape, lambda b, i: (0, 0)),
            _resident(fi.shape, lambda b, i: (0, 0)),
            _resident(d_row.shape, lambda b, i: (0, 0)),
            _resident(w_glu_bf.shape, lambda b, i: (0, 0)),
            _resident(b_glu_row.shape, lambda b, i: (0, 0)),
        ],
        out_specs=pl.BlockSpec((1, SSM_TM, SSM_WIDTH), lambda b, i: (b, jnp.maximum(i - 1, 0), 0)),
        out_shape=jax.ShapeDtypeStruct((bsz, seq, SSM_WIDTH), jnp.bfloat16),
        scratch_shapes=[pltpu.VMEM((SSM_TM, 2 * N_STATE), jnp.float32),
                        pltpu.VMEM((SUBLANES, 2 * N_STATE), jnp.float32)],
        compiler_params=pltpu.CompilerParams(
            dimension_semantics=("arbitrary", "arbitrary"),
            vmem_limit_bytes=VMEM_LIMIT),
        name="ssm",
    )(proj, proj, bw, cw, mr, mi, fr, fi, d_row, w_glu_bf, b_glu_row)


def _attention_kernel(q_ref, k_ref, v_ref, g_ref, tri_ref, o_ref, acc_ref, run_ref):
    qi = pl.program_id(2)
    t = ATT_T
    qs = FRONT + qi * t
    first_key = FRONT - N_META
    n_blocks = qi + 2
    scale = 1.0 / math.sqrt(SB_HEAD_DIM)

    acc_ref[...] = jnp.zeros_like(acc_ref)
    run_ref[...] = jnp.zeros_like(run_ref)
    q = q_ref[0]

    def cond(state):
        j, live = state
        return jnp.logical_and(j < n_blocks, live)

    def body(state):
        j, _ = state
        ks = pl.multiple_of(qs - j * t, t)
        kb = k_ref[0, pl.ds(ks, t), :]
        vb = v_ref[0, pl.ds(ks, t), :]
        z = lax.dot_general(q, kb, (((1,), (1,)), ((), ())),
                            preferred_element_type=jnp.float32) * scale
        q_pos = qs + lax.broadcasted_iota(jnp.int32, (t, t), 0)
        k_pos = ks + lax.broadcasted_iota(jnp.int32, (t, t), 1)
        mask = jnp.logical_and(k_pos < q_pos, k_pos >= first_key)
        softplus = jnp.maximum(z, 0.0) + jnp.log1p(jnp.exp(-jnp.abs(z)))
        lk = jnp.where(mask, -softplus, 0.0)
        hi = lk.astype(jnp.bfloat16)
        lo = (lk - hi.astype(jnp.float32)).astype(jnp.bfloat16)
        sums = jnp.dot(jnp.concatenate([hi, lo], axis=1), tri_ref[...],
                       preferred_element_type=jnp.float32)
        run = run_ref[...]
        suffix = sums[:, :t] + jnp.concatenate([run] * (t // LANES), axis=1)
        a = jnp.where(mask, jnp.exp(z + lk + suffix), 0.0)
        acc_ref[...] += jnp.dot(a.astype(jnp.bfloat16), vb, preferred_element_type=jnp.float32)
        run = run + sums[:, t:]
        run_ref[...] = run
        return j + 1, jnp.max(run) >= EXP_ZERO_BELOW

    lax.while_loop(cond, body, (jnp.int32(0), jnp.bool_(True)))
    g = g_ref[0].astype(jnp.float32)
    o_ref[0] = (acc_ref[...] * (g * _sigmoid(g))).astype(o_ref.dtype)


def _attention(proj, tri, seq):
    bsz, lp, _ = proj.shape
    t = ATT_T
    hd = SB_HEAD_DIM
    q_col0 = 2 * SSM_WIDTH // hd
    k_col0 = q_col0 + SB_HEADS
    v_col0 = k_col0 + SB_HEADS
    g_col0 = v_col0 + SB_HEADS
    return pl.pallas_call(
        _attention_kernel,
        grid=(bsz, SB_HEADS, seq // t),
        in_specs=[
            pl.BlockSpec((1, t, hd), lambda b, h, i: (b, FRONT // t + i, q_col0 + h)),
            pl.BlockSpec((1, lp, hd), lambda b, h, i: (b, 0, k_col0 + h)),
            pl.BlockSpec((1, lp, hd), lambda b, h, i: (b, 0, v_col0 + h)),
            pl.BlockSpec((1, t, hd), lambda b, h, i: (b, FRONT // t + i, g_col0 + h)),
            _resident(tri.shape, lambda b, h, i: (0, 0)),
        ],
        out_specs=pl.BlockSpec((1, t, hd), lambda b, h, i: (b, i, h)),
        out_shape=jax.ShapeDtypeStruct((bsz, seq, SB_WIDTH), jnp.bfloat16),
        scratch_shapes=[pltpu.VMEM((t, hd), jnp.float32),
                        pltpu.VMEM((t, LANES), jnp.float32)],
        compiler_params=pltpu.CompilerParams(
            dimension_semantics=("arbitrary", "arbitrary", "arbitrary"),
            vmem_limit_bytes=VMEM_LIMIT),
        name="attention",
    )(proj, proj, proj, proj, tri)


def _mix_out_kernel(x_ref, a_ref, s_ref, wa_ref, ws_ref, wg1_ref, wg2_ref, bg1_ref, bg2_ref,
                    wo_ref, gain_ref, bias_ref, o_ref, xb_ref, acc_ref):
    n = pl.program_id(2)

    @pl.when(n == 0)
    def _():
        xb_ref[...] = x_ref[0].astype(jnp.bfloat16)
        acc_ref[...] = jnp.zeros_like(acc_ref)

    f32 = jnp.float32
    xb = xb_ref[...]
    y_ssm = jnp.dot(a_ref[0], wa_ref[...], preferred_element_type=f32)
    y_sb = jnp.dot(s_ref[0], ws_ref[...], preferred_element_type=f32)
    g1 = _sigmoid(jnp.dot(xb, wg1_ref[...], preferred_element_type=f32) + bg1_ref[...])
    g2 = _sigmoid(jnp.dot(xb, wg2_ref[...], preferred_element_type=f32) + bg2_ref[...])
    mixed = (g1 * y_ssm + g2 * y_sb).astype(jnp.bfloat16)
    acc_ref[...] += jnp.dot(mixed, wo_ref[...], preferred_element_type=f32)

    @pl.when(n == pl.num_programs(2) - 1)
    def _():
        r = DEEPNORM_ALPHA * x_ref[0] + acc_ref[...]
        mu = jnp.mean(r, axis=-1, keepdims=True)
        c = r - mu
        var = jnp.mean(c * c, axis=-1, keepdims=True)
        o_ref[0] = c * lax.rsqrt(var + LN_EPS) * gain_ref[...] + bias_ref[...]


def _mix_out(x, a_ssm, a_sb, wa, ws, wg, bg_row, wo, gain_row, bias_row):
    bsz, seq, _ = x.shape
    n_col = D_MODEL // MIX_TN
    return pl.pallas_call(
        _mix_out_kernel,
        grid=(bsz, seq // MIX_TM, n_col),
        in_specs=[
            pl.BlockSpec((1, MIX_TM, D_MODEL), lambda b, j, n: (b, j, 0)),
            pl.BlockSpec((1, MIX_TM, SSM_WIDTH), lambda b, j, n: (b, j, 0)),
            pl.BlockSpec((1, MIX_TM, SB_WIDTH), lambda b, j, n: (b, j, 0)),
            pl.BlockSpec((SSM_WIDTH, MIX_TN), lambda b, j, n: (0, n)),
            pl.BlockSpec((SB_WIDTH, MIX_TN), lambda b, j, n: (0, n)),
            pl.BlockSpec((D_MODEL, MIX_TN), lambda b, j, n: (0, n)),
            pl.BlockSpec((D_MODEL, MIX_TN), lambda b, j, n: (0, n_col + n)),
            pl.BlockSpec((1, MIX_TN), lambda b, j, n: (0, n)),
            pl.BlockSpec((1, MIX_TN), lambda b, j, n: (0, n_col + n)),
            pl.BlockSpec((MIX_TN, D_MODEL), lambda b, j, n: (n, 0)),
            _resident((1, D_MODEL), lambda b, j, n: (0, 0)),
            _resident((1, D_MODEL), lambda b, j, n: (0, 0)),
        ],
        out_specs=pl.BlockSpec((1, MIX_TM, D_MODEL), lambda b, j, n: (b, j, 0)),
        out_shape=jax.ShapeDtypeStruct((bsz, seq, D_MODEL), jnp.float32),
        scratch_shapes=[pltpu.VMEM((MIX_TM, D_MODEL), jnp.bfloat16),
                        pltpu.VMEM((MIX_TM, D_MODEL), jnp.float32)],
        compiler_params=pltpu.CompilerParams(
            dimension_semantics=("arbitrary", "arbitrary", "arbitrary"),
            vmem_limit_bytes=VMEM_LIMIT),
        name="mix_out",
    )(x, a_ssm, a_sb, wa, ws, wg, wg, bg_row, bg_row, wo, gain_row, bias_row)


def _block_diag_b(bb_r, bb_i):
    gl = SSM_GROUPS // SSM_CHUNKS
    eye = jnp.eye(gl, dtype=jnp.float32)

    def part(bb):
        b4 = jnp.transpose(bb, (1, 0, 2)).reshape(SSM_CHUNKS, gl, SSM_GROUP, SSM_STATE)
        return jnp.einsum('kgcp,gh->kgchp', b4, eye).reshape(SSM_CHUNKS, CHUNK_CH, CHUNK_ST)

    return jnp.concatenate([part(bb_r), part(bb_i)], axis=2).astype(jnp.bfloat16)


def _block_diag_c(c_re, c_im):
    gl = SSM_GROUPS // SSM_CHUNKS
    eye = jnp.eye(gl, dtype=jnp.float32)

    def part(c):
        c4 = c.astype(jnp.float32).reshape(SSM_CHUNKS, gl, SSM_GROUP, SSM_STATE)
        return jnp.einsum('kgcp,gh->kgphc', c4, eye).reshape(SSM_CHUNKS, CHUNK_ST, CHUNK_CH)

    return jnp.concatenate([part(c_re), -part(c_im)], axis=1).astype(jnp.bfloat16)


def _scan_tables(pw_r, pw_i):
    pr = pw_r.reshape(N_POW, N_STATE)
    pi = pw_i.reshape(N_POW, N_STATE)
    rows = jnp.arange(SUBLANES)[:, None]
    mr = jnp.stack([jnp.where(rows >= d, pr[d - 1][None, :], 0.0) for d in SCAN_STEPS])
    mi = jnp.stack([jnp.where(rows >= d, pi[d - 1][None, :], 0.0) for d in SCAN_STEPS])
    return mr, mi, pr, pi


def kernel(x, meta_tokens, w_in, ssm_lambda_re, ssm_lambda_im, ssm_log_dt, ssm_b_re, ssm_b_im,
           ssm_c_re, ssm_c_im, ssm_d, w_glu, b_glu, w_branch_ssm, w_branch_sb, w_gate, b_gate,
           w_out, ln_gain, ln_bias):
    bsz, seq, _ = x.shape
    f32 = jnp.float32
    bf16 = jnp.bfloat16
    layer = 0

    bb_r, bb_i, pw_r, pw_i = _ssm_params(ssm_lambda_re[layer], ssm_lambda_im[layer],
                                         ssm_log_dt[layer], ssm_b_re[layer], ssm_b_im[layer])
    bw = _block_diag_b(bb_r, bb_i)
    cw = _block_diag_c(ssm_c_re[layer], ssm_c_im[layer])
    mr, mi, fr, fi = _scan_tables(pw_r, pw_i)

    meta_block = jnp.concatenate(
        [jnp.zeros((META_ROWS - N_META, D_MODEL), f32), meta_tokens.astype(f32)], axis=0)
    proj = _in_proj(x.astype(f32), meta_block, w_in[layer].astype(bf16))

    a_ssm = _ssm(proj, bw, cw, mr, mi, fr, fi,
                 ssm_d[layer].astype(f32).reshape(1, SSM_WIDTH),
                 w_glu[layer].astype(bf16), b_glu[layer].astype(f32).reshape(1, 2 * SSM_WIDTH), seq)

    t = ATT_T
    r = jnp.arange(t)
    lower = (r[:, None] > r[None, :]).astype(bf16)
    tri_one = jnp.concatenate([lower, jnp.ones((t, LANES), bf16)], axis=1)
    tri = jnp.concatenate([tri_one, tri_one], axis=0)
    a_sb = _attention(proj, tri, seq)

    return _mix_out(x.astype(f32), a_ssm, a_sb,
                    w_branch_ssm[layer].astype(bf16), w_branch_sb[layer].astype(bf16),
                    w_gate[layer].astype(bf16), b_gate[layer].astype(f32).reshape(1, 2 * D_MODEL),
                    w_out[layer].astype(bf16),
                    ln_gain[layer].astype(f32).reshape(1, D_MODEL),
                    ln_bias[layer].astype(f32).reshape(1, D_MODEL))
```

```python
import math

import jax
import jax.numpy as jnp
from jax import lax
from jax.experimental import pallas as pl
from jax.experimental.pallas import tpu as pltpu

D_MODEL = 2048
N_META = 16
SSM_WIDTH = 1024
SSM_GROUP = 16
SSM_GROUPS = 64
SSM_STATE = 64
SB_WIDTH = 1024
SB_HEADS = 8
SB_HEAD_DIM = 128
IN_WIDTH = 2 * SSM_WIDTH + 4 * SB_WIDTH
DEEPNORM_ALPHA = 2.0 ** 0.25
LN_EPS = 1e-5

SUBLANES = 8
LANES = 128

FRONT = 1024
PROJ_TM = 1024
PROJ_TN = 1024
Q_SLAB = 2 * SSM_WIDTH // PROJ_TN
SSM_TM = 256
ATT_T = 256
ATT_HEADS = 2
MIX_TM = 512
MIX_TN = 512
META_ROWS = 256

SSM_CHUNKS = 4
CHUNK_CH = SSM_WIDTH // SSM_CHUNKS
CHUNK_ST = SSM_GROUPS * SSM_STATE // SSM_CHUNKS
N_STATE = SSM_GROUPS * SSM_STATE
SEG_LEN = SSM_TM // SUBLANES
SEG_STEPS = (1, 2, 4)
N_POW = SEG_LEN + SUBLANES - 1

EXP_ZERO_BELOW = -104.0
MASKED_SCORE = -1e30

VMEM_LIMIT = 56 * 1024 * 1024


def _resident(block_shape, index_map):
    return pl.BlockSpec(block_shape, index_map, pipeline_mode=pl.Buffered(1))


def _cmul(ar, ai, br, bi):
    return ar * br - ai * bi, ar * bi + ai * br


def _ssm_params_kernel(lr_ref, li_ref, logdt_ref, br_ref, bi_ref,
                       bbr_ref, bbi_ref, pwr_ref, pwi_ref):
    lr = lr_ref[...]
    li = li_ref[...]
    dt = jnp.exp(logdt_ref[...])
    mag = jnp.exp(lr * dt)
    ab_r = mag * jnp.cos(li * dt)
    ab_i = mag * jnp.sin(li * dt)
    pr, pi = ab_r, ab_i
    pwr_ref[0] = pr
    pwi_ref[0] = pi
    for e in range(1, SEG_LEN):
        pr, pi = _cmul(pr, pi, ab_r, ab_i)
        pwr_ref[e] = pr
        pwi_ref[e] = pi
    sr, si = pr, pi
    for k in range(1, SUBLANES):
        sr, si = _cmul(sr, si, pr, pi)
        pwr_ref[SEG_LEN - 1 + k] = sr
        pwi_ref[SEG_LEN - 1 + k] = si
    den = lr * lr + li * li
    nr = ab_r - 1.0
    f_r = (nr * lr + ab_i * li) / den
    f_i = (ab_i * lr - nr * li) / den
    br = br_ref[...]
    bi = bi_ref[...]
    bbr_ref[...] = f_r[None] * br - f_i[None] * bi
    bbi_ref[...] = f_r[None] * bi + f_i[None] * br


def _ssm_params(lam_re, lam_im, log_dt, b_re, b_im):
    g, p, c = SSM_GROUPS, SSM_STATE, SSM_GROUP
    f32 = jnp.float32
    return pl.pallas_call(
        _ssm_params_kernel,
        out_shape=(jax.ShapeDtypeStruct((c, g, p), f32),
                   jax.ShapeDtypeStruct((c, g, p), f32),
                   jax.ShapeDtypeStruct((N_POW, g, p), f32),
                   jax.ShapeDtypeStruct((N_POW, g, p), f32)),
        name="ssm_params",
    )(lam_re.astype(f32), lam_im.astype(f32), log_dt.astype(f32).reshape(g, 1),
      jnp.transpose(b_re.astype(f32), (2, 0, 1)), jnp.transpose(b_im.astype(f32), (2, 0, 1)))


def _in_proj_kernel(x_ref, meta_ref, w_ref, o_ref, xb_ref):
    j = pl.program_id(1)
    n = pl.program_id(2)

    @pl.when(jnp.logical_and(n == 0, j == 0))
    def _():
        xb_ref[0:PROJ_TM - META_ROWS, :] = jnp.zeros((PROJ_TM - META_ROWS, D_MODEL), jnp.bfloat16)
        xb_ref[PROJ_TM - META_ROWS:PROJ_TM, :] = meta_ref[...].astype(jnp.bfloat16)

    @pl.when(jnp.logical_and(n == 0, j > 0))
    def _():
        xb_ref[...] = x_ref[0].astype(jnp.bfloat16)

    acc = jnp.dot(xb_ref[...], w_ref[...], preferred_element_type=jnp.float32)
    scale = jnp.where(n == Q_SLAB, 1.0 / math.sqrt(SB_HEAD_DIM), 1.0).astype(jnp.float32)
    o_ref[0] = (acc * scale).astype(o_ref.dtype)


def _in_proj(x, meta_block, w_in_bf):
    bsz, seq, _ = x.shape
    n_tiles = seq // PROJ_TM + FRONT // PROJ_TM
    lp = seq + FRONT
    return pl.pallas_call(
        _in_proj_kernel,
        grid=(bsz, n_tiles, IN_WIDTH // PROJ_TN),
        in_specs=[
            pl.BlockSpec((1, PROJ_TM, D_MODEL), lambda b, j, n: (b, jnp.maximum(j - 1, 0), 0)),
            _resident((META_ROWS, D_MODEL), lambda b, j, n: (0, 0)),
            pl.BlockSpec((D_MODEL, PROJ_TN), lambda b, j, n: (0, n)),
        ],
        out_specs=pl.BlockSpec((1, PROJ_TM, PROJ_TN), lambda b, j, n: (b, j, n)),
        out_shape=jax.ShapeDtypeStruct((bsz, lp, IN_WIDTH), jnp.bfloat16),
        scratch_shapes=[pltpu.VMEM((PROJ_TM, D_MODEL), jnp.bfloat16)],
        compiler_params=pltpu.CompilerParams(
            dimension_semantics=("arbitrary", "arbitrary", "arbitrary"),
            vmem_limit_bytes=VMEM_LIMIT),
        name="in_proj",
    )(x, meta_block, w_in_bf)


def _gelu_tanh(x):
    return 0.5 * x * (1.0 + jnp.tanh(math.sqrt(2.0 / math.pi) * (x + 0.044715 * (x * x * x))))


def _sigmoid(x):
    return 1.0 / (1.0 + jnp.exp(-x))


def _ssm_kernel(u_ref, g_ref, perm_ref, permt_ref, bw_ref, cw_ref, a1r_ref, a1i_ref,
                sgr_ref, sgi_ref, sfr_ref, sfi_ref, fr_ref, fi_ref,
                d_ref, wglu_ref, bglu_ref, o_ref, s_ref, carry_ref):
    i = pl.program_id(1)
    f32 = jnp.float32
    bf16 = jnp.bfloat16

    @pl.when(i == 0)
    def _():
        carry_ref[...] = jnp.zeros_like(carry_ref)

    perm = perm_ref[...]
    u_p = jnp.dot(perm, u_ref[0], preferred_element_type=f32).astype(bf16)
    g_p = jnp.dot(perm, g_ref[0], preferred_element_type=f32)

    for kc in range(SSM_CHUNKS):
        s_ref[:, kc * 2 * CHUNK_ST:(kc + 1) * 2 * CHUNK_ST] = jnp.dot(
            u_p[:, kc * CHUNK_CH:(kc + 1) * CHUNK_CH], bw_ref[kc], preferred_element_type=f32)

    lane_tiles = CHUNK_ST // LANES
    for kc in range(SSM_CHUNKS):
        cols = [(kc * 2 * CHUNK_ST + c * LANES,
                 kc * 2 * CHUNK_ST + CHUNK_ST + c * LANES,
                 kc * CHUNK_ST + c * LANES) for c in range(lane_tiles)]

        def body(t, state, cols=cols):
            r0 = pl.multiple_of(t * SUBLANES, SUBLANES)
            new = []
            for idx, (cr, ci, cm) in enumerate(cols):
                ar = a1r_ref[:, cm:cm + LANES]
                ai = a1i_ref[:, cm:cm + LANES]
                xr = state[2 * idx]
                xi = state[2 * idx + 1]
                nr = (ar * xr - ai * xi) + s_ref[pl.ds(r0, SUBLANES), cr:cr + LANES]
                ni = (ar * xi + ai * xr) + s_ref[pl.ds(r0, SUBLANES), ci:ci + LANES]
                s_ref[pl.ds(r0, SUBLANES), cr:cr + LANES] = nr
                s_ref[pl.ds(r0, SUBLANES), ci:ci + LANES] = ni
                new.append(nr)
                new.append(ni)
            return tuple(new)

        zero = jnp.zeros((SUBLANES, LANES), f32)
        lax.fori_loop(0, SEG_LEN, body, tuple(zero for _ in range(2 * lane_tiles)))

    row = lax.broadcasted_iota(jnp.int32, (SUBLANES, CHUNK_ST), 0)
    ys = []
    for kc in range(SSM_CHUNKS):
        re0 = kc * 2 * CHUNK_ST
        im0 = re0 + CHUNK_ST
        cm0 = kc * CHUNK_ST
        last = SSM_TM - SUBLANES
        er = s_ref[last:SSM_TM, re0:re0 + CHUNK_ST]
        ei = s_ref[last:SSM_TM, im0:im0 + CHUNK_ST]
        for k, d in enumerate(SEG_STEPS):
            mr = sgr_ref[k, :, cm0:cm0 + CHUNK_ST]
            mi = sgi_ref[k, :, cm0:cm0 + CHUNK_ST]
            pr = pltpu.roll(er, d, 0)
            pi = pltpu.roll(ei, d, 0)
            er, ei = er + (mr * pr - mi * pi), ei + (mr * pi + mi * pr)
        c_r = carry_ref[:, re0:re0 + CHUNK_ST]
        c_i = carry_ref[:, im0:im0 + CHUNK_ST]
        sfr = sfr_ref[:, cm0:cm0 + CHUNK_ST]
        sfi = sfi_ref[:, cm0:cm0 + CHUNK_ST]
        hr = er + (sfr * c_r - sfi * c_i)
        hi = ei + (sfr * c_i + sfi * c_r)
        in_r = jnp.where(row == 0, c_r, pltpu.roll(hr, 1, 0))
        in_i = jnp.where(row == 0, c_i, pltpu.roll(hi, 1, 0))
        carry_ref[:, re0:re0 + CHUNK_ST] = jnp.broadcast_to(hr[SUBLANES - 1:SUBLANES, :],
                                                             (SUBLANES, CHUNK_ST))
        carry_ref[:, im0:im0 + CHUNK_ST] = jnp.broadcast_to(hi[SUBLANES - 1:SUBLANES, :],
                                                             (SUBLANES, CHUNK_ST))
        tin_r = jnp.concatenate([in_r] * SEG_LEN, axis=0)
        tin_i = jnp.concatenate([in_i] * SEG_LEN, axis=0)
        fr = fr_ref[:, cm0:cm0 + CHUNK_ST]
        fi = fi_ref[:, cm0:cm0 + CHUNK_ST]
        h_r = s_ref[:, re0:re0 + CHUNK_ST] + (fr * tin_r - fi * tin_i)
        h_i = s_ref[:, im0:im0 + CHUNK_ST] + (fr * tin_i + fi * tin_r)
        h_bf = jnp.concatenate([h_r.astype(bf16), h_i.astype(bf16)], axis=1)
        ys.append(jnp.dot(h_bf, cw_ref[kc], preferred_element_type=f32))
    y = jnp.concatenate(ys, axis=1) + d_ref[...] * u_p.astype(f32)
    z = _gelu_tanh(y).astype(bf16)
    hg = jnp.dot(z, wglu_ref[...], preferred_element_type=f32) + bglu_ref[...]
    glu = hg[:, :SSM_WIDTH] * _sigmoid(hg[:, SSM_WIDTH:])
    out_p = (glu * (g_p * _sigmoid(g_p))).astype(bf16)
    o_ref[0] = jnp.dot(permt_ref[...], out_p, preferred_element_type=f32).astype(o_ref.dtype)


def _ssm(proj, perm, perm_t, bw, cw, tables, d_row, w_glu_bf, b_glu_row, seq):
    bsz = proj.shape[0]
    t0 = FRONT // SSM_TM - 1
    n_tiles = seq // SSM_TM + 1
    consts = (perm, perm_t, bw, cw) + tuple(tables) + (d_row, w_glu_bf, b_glu_row)

    def const_spec(a):
        nd = a.ndim
        return _resident(a.shape, lambda b, i, nd=nd: (0,) * nd)

    return pl.pallas_call(
        _ssm_kernel,
        grid=(bsz, n_tiles),
        in_specs=[
            pl.BlockSpec((1, SSM_TM, SSM_WIDTH), lambda b, i: (b, t0 + i, 0)),
            pl.BlockSpec((1, SSM_TM, SSM_WIDTH), lambda b, i: (b, t0 + i, 1)),
        ] + [const_spec(a) for a in consts],
        out_specs=pl.BlockSpec((1, SSM_TM, SSM_WIDTH), lambda b, i: (b, jnp.maximum(i - 1, 0), 0)),
        out_shape=jax.ShapeDtypeStruct((bsz, seq, SSM_WIDTH), jnp.bfloat16),
        scratch_shapes=[pltpu.VMEM((SSM_TM, 2 * N_STATE), jnp.float32),
                        pltpu.VMEM((SUBLANES, 2 * N_STATE), jnp.float32)],
        compiler_params=pltpu.CompilerParams(
            dimension_semantics=("arbitrary", "arbitrary"),
            vmem_limit_bytes=VMEM_LIMIT),
        name="ssm",
    )(proj, proj, *consts)


def _attention_kernel(q_ref, k_ref, v_ref, g_ref, tri_ref, o_ref, acc_ref, run_ref):
    qi = pl.program_id(2)
    t = ATT_T
    hd = SB_HEAD_DIM
    f32 = jnp.float32
    bf16 = jnp.bfloat16
    qs = FRONT + qi * t
    first_key = FRONT - N_META
    n_blocks = qi + 2
    row = lax.broadcasted_iota(jnp.int32, (t, t), 0)
    col = lax.broadcasted_iota(jnp.int32, (t, t), 1)

    def block(h, ks, run, causal):
        hs = slice(h * hd, (h + 1) * hd)
        q = q_ref[0, :, hs]
        kb = k_ref[0, pl.ds(ks, t), hs]
        vb = v_ref[0, pl.ds(ks, t), hs]
        z = lax.dot_general(q, kb, (((1,), (1,)), ((), ())), preferred_element_type=f32)
        if causal:
            z = jnp.where(col < row, z, MASKED_SCORE)
        else:
            z = jnp.where(col >= first_key - ks, z, MASKED_SCORE)
        sp = jnp.maximum(z, 0.0) + jnp.log(1.0 + jnp.exp(-jnp.abs(z)))
        hi = sp.astype(bf16)
        lo = (sp - hi.astype(f32)).astype(bf16)
        nsum = jnp.dot(jnp.concatenate([hi, lo], axis=1), tri_ref[...],
                       preferred_element_type=f32)
        a = jnp.exp(z + nsum + jnp.concatenate([run] * (t // LANES), axis=1))
        pv = jnp.dot(a.astype(bf16), vb, preferred_element_type=f32)
        return pv, run + jnp.broadcast_to(nsum[:, 0:1], (t, LANES))

    def pair(h, ks, run, first):
        pv0, run = block(h, pl.multiple_of(ks, t), run, first)
        pv1, run = block(h, pl.multiple_of(ks - t, t), run, False)
        return pv0 + pv1, run

    live = jnp.bool_(False)
    for h in range(ATT_HEADS):
        lanes = slice(h * LANES, (h + 1) * LANES)
        pv, run = pair(h, qs, jnp.zeros((t, LANES), f32), True)
        acc_ref[:, lanes] = pv
        run_ref[:, lanes] = run
        live = jnp.logical_or(live, jnp.max(run) >= EXP_ZERO_BELOW)

    def cond(state):
        j, live = state
        return jnp.logical_and(j < n_blocks, live)

    def body(state):
        j, _ = state
        live = jnp.bool_(False)
        for h in range(ATT_HEADS):
            lanes = slice(h * LANES, (h + 1) * LANES)
            pv, run = pair(h, qs - j * t, run_ref[:, lanes], False)
            acc_ref[:, lanes] += pv
            run_ref[:, lanes] = run
            live = jnp.logical_or(live, jnp.max(run) >= EXP_ZERO_BELOW)
        return j + 2, live

    lax.while_loop(cond, body, (jnp.int32(2), live))
    g = g_ref[0].astype(f32)
    o_ref[0] = (acc_ref[...] * (g * _sigmoid(g))).astype(o_ref.dtype)


def _attention(proj, tri, seq):
    bsz, lp, _ = proj.shape
    t = ATT_T
    w = ATT_HEADS * SB_HEAD_DIM
    q_col0 = 2 * SSM_WIDTH // w
    k_col0 = q_col0 + SB_WIDTH // w
    v_col0 = k_col0 + SB_WIDTH // w
    g_col0 = v_col0 + SB_WIDTH // w
    return pl.pallas_call(
        _attention_kernel,
        grid=(bsz, SB_HEADS // ATT_HEADS, seq // t),
        in_specs=[
            pl.BlockSpec((1, t, w), lambda b, h, i: (b, FRONT // t + i, q_col0 + h)),
            pl.BlockSpec((1, lp, w), lambda b, h, i: (b, 0, k_col0 + h)),
            pl.BlockSpec((1, lp, w), lambda b, h, i: (b, 0, v_col0 + h)),
            pl.BlockSpec((1, t, w), lambda b, h, i: (b, FRONT // t + i, g_col0 + h)),
            _resident(tri.shape, lambda b, h, i: (0, 0)),
        ],
        out_specs=pl.BlockSpec((1, t, w), lambda b, h, i: (b, i, h)),
        out_shape=jax.ShapeDtypeStruct((bsz, seq, SB_WIDTH), jnp.bfloat16),
        scratch_shapes=[pltpu.VMEM((t, w), jnp.float32),
                        pltpu.VMEM((t, ATT_HEADS * LANES), jnp.float32)],
        compiler_params=pltpu.CompilerParams(
            dimension_semantics=("arbitrary", "arbitrary", "arbitrary"),
            vmem_limit_bytes=VMEM_LIMIT),
        name="attention",
    )(proj, proj, proj, proj, tri)


def _mix_out_kernel(x_ref, a_ref, s_ref, wa_ref, ws_ref, wg1_ref, wg2_ref, bg1_ref, bg2_ref,
                    wo_ref, gain_ref, bias_ref, o_ref, xb_ref, acc_ref):
    n = pl.program_id(2)

    @pl.when(n == 0)
    def _():
        xb_ref[...] = x_ref[0].astype(jnp.bfloat16)
        acc_ref[...] = jnp.zeros_like(acc_ref)

    f32 = jnp.float32
    xb = xb_ref[...]
    y_ssm = jnp.dot(a_ref[0], wa_ref[...], preferred_element_type=f32)
    y_sb = jnp.dot(s_ref[0], ws_ref[...], preferred_element_type=f32)
    g1 = _sigmoid(jnp.dot(xb, wg1_ref[...], preferred_element_type=f32) + bg1_ref[...])
    g2 = _sigmoid(jnp.dot(xb, wg2_ref[...], preferred_element_type=f32) + bg2_ref[...])
    mixed = (g1 * y_ssm + g2 * y_sb).astype(jnp.bfloat16)
    acc_ref[...] += jnp.dot(mixed, wo_ref[...], preferred_element_type=f32)

    @pl.when(n == pl.num_programs(2) - 1)
    def _():
        r = DEEPNORM_ALPHA * x_ref[0] + acc_ref[...]
        mu = jnp.mean(r, axis=-1, keepdims=True)
        c = r - mu
        var = jnp.mean(c * c, axis=-1, keepdims=True)
        o_ref[0] = c * lax.rsqrt(var + LN_EPS) * gain_ref[...] + bias_ref[...]


def _mix_out(x, a_ssm, a_sb, wa, ws, wg, bg_row, wo, gain_row, bias_row):
    bsz, seq, _ = x.shape
    n_col = D_MODEL // MIX_TN
    return pl.pallas_call(
        _mix_out_kernel,
        grid=(bsz, seq // MIX_TM, n_col),
        in_specs=[
            pl.BlockSpec((1, MIX_TM, D_MODEL), lambda b, j, n: (b, j, 0)),
            pl.BlockSpec((1, MIX_TM, SSM_WIDTH), lambda b, j, n: (b, j, 0)),
            pl.BlockSpec((1, MIX_TM, SB_WIDTH), lambda b, j, n: (b, j, 0)),
            pl.BlockSpec((SSM_WIDTH, MIX_TN), lambda b, j, n: (0, n)),
            pl.BlockSpec((SB_WIDTH, MIX_TN), lambda b, j, n: (0, n)),
            pl.BlockSpec((D_MODEL, MIX_TN), lambda b, j, n: (0, n)),
            pl.BlockSpec((D_MODEL, MIX_TN), lambda b, j, n: (0, n_col + n)),
            pl.BlockSpec((1, MIX_TN), lambda b, j, n: (0, n)),
            pl.BlockSpec((1, MIX_TN), lambda b, j, n: (0, n_col + n)),
            pl.BlockSpec((MIX_TN, D_MODEL), lambda b, j, n: (n, 0)),
            _resident((1, D_MODEL), lambda b, j, n: (0, 0)),
            _resident((1, D_MODEL), lambda b, j, n: (0, 0)),
        ],
        out_specs=pl.BlockSpec((1, MIX_TM, D_MODEL), lambda b, j, n: (b, j, 0)),
        out_shape=jax.ShapeDtypeStruct((bsz, seq, D_MODEL), jnp.float32),
        scratch_shapes=[pltpu.VMEM((MIX_TM, D_MODEL), jnp.bfloat16),
                        pltpu.VMEM((MIX_TM, D_MODEL), jnp.float32)],
        compiler_params=pltpu.CompilerParams(
            dimension_semantics=("arbitrary", "arbitrary", "arbitrary"),
            vmem_limit_bytes=VMEM_LIMIT),
        name="mix_out",
    )(x, a_ssm, a_sb, wa, ws, wg, wg, bg_row, bg_row, wo, gain_row, bias_row)


def _block_diag_b(bb_r, bb_i):
    gl = SSM_GROUPS // SSM_CHUNKS
    eye = jnp.eye(gl, dtype=jnp.float32)

    def part(bb):
        b4 = jnp.transpose(bb, (1, 0, 2)).reshape(SSM_CHUNKS, gl, SSM_GROUP, SSM_STATE)
        return jnp.einsum('kgcp,gh->kgchp', b4, eye).reshape(SSM_CHUNKS, CHUNK_CH, CHUNK_ST)

    return jnp.concatenate([part(bb_r), part(bb_i)], axis=2).astype(jnp.bfloat16)


def _block_diag_c(c_re, c_im):
    gl = SSM_GROUPS // SSM_CHUNKS
    eye = jnp.eye(gl, dtype=jnp.float32)

    def part(c):
        c4 = c.astype(jnp.float32).reshape(SSM_CHUNKS, gl, SSM_GROUP, SSM_STATE)
        return jnp.einsum('kgcp,gh->kgphc', c4, eye).reshape(SSM_CHUNKS, CHUNK_ST, CHUNK_CH)

    return jnp.concatenate([part(c_re), -part(c_im)], axis=1).astype(jnp.bfloat16)


def _scan_tables(pw_r, pw_i):
    rows = jnp.arange(SUBLANES)[:, None]

    def tables(pw):
        p = pw.reshape(N_POW, N_STATE)
        a1 = jnp.broadcast_to(p[0][None, :], (SUBLANES, N_STATE))
        seg = p[SEG_LEN - 1:]
        sg = jnp.stack([jnp.where(rows >= d, seg[d - 1][None, :], 0.0) for d in SEG_STEPS])
        f = jnp.repeat(p[:SEG_LEN], SUBLANES, axis=0)
        return a1, sg, seg, f

    a1r, sgr, sfr, fr = tables(pw_r)
    a1i, sgi, sfi, fi = tables(pw_i)
    return a1r, a1i, sgr, sgi, sfr, sfi, fr, fi


def _scan_permutation():
    r = jnp.arange(SSM_TM)
    tok = (r % SUBLANES) * SEG_LEN + r // SUBLANES
    perm = (tok[:, None] == jnp.arange(SSM_TM)[None, :]).astype(jnp.bfloat16)
    return perm, perm.T


def kernel(x, meta_tokens, w_in, ssm_lambda_re, ssm_lambda_im, ssm_log_dt, ssm_b_re, ssm_b_im,
           ssm_c_re, ssm_c_im, ssm_d, w_glu, b_glu, w_branch_ssm, w_branch_sb, w_gate, b_gate,
           w_out, ln_gain, ln_bias):
    bsz, seq, _ = x.shape
    f32 = jnp.float32
    bf16 = jnp.bfloat16
    layer = 0

    bb_r, bb_i, pw_r, pw_i = _ssm_params(ssm_lambda_re[layer], ssm_lambda_im[layer],
                                         ssm_log_dt[layer], ssm_b_re[layer], ssm_b_im[layer])
    bw = _block_diag_b(bb_r, bb_i)
    cw = _block_diag_c(ssm_c_re[layer], ssm_c_im[layer])
    tables = _scan_tables(pw_r, pw_i)
    perm, perm_t = _scan_permutation()

    meta_block = jnp.concatenate(
        [jnp.zeros((META_ROWS - N_META, D_MODEL), f32), meta_tokens.astype(f32)], axis=0)
    proj = _in_proj(x.astype(f32), meta_block, w_in[layer].astype(bf16))

    a_ssm = _ssm(proj, perm, perm_t, bw, cw, tables,
                 ssm_d[layer].astype(f32).reshape(1, SSM_WIDTH),
                 w_glu[layer].astype(bf16), b_glu[layer].astype(f32).reshape(1, 2 * SSM_WIDTH), seq)

    t = ATT_T
    r = jnp.arange(t)
    neg_lower = -(r[:, None] >= r[None, :]).astype(bf16)
    tri = jnp.concatenate([neg_lower, neg_lower], axis=0)
    a_sb = _attention(proj, tri, seq)

    return _mix_out(x.astype(f32), a_ssm, a_sb,
                    w_branch_ssm[layer].astype(bf16), w_branch_sb[layer].astype(bf16),
                    w_gate[layer].astype(bf16), b_gate[layer].astype(f32).reshape(1, 2 * D_MODEL),
                    w_out[layer].astype(bf16),
                    ln_gain[layer].astype(f32).reshape(1, D_MODEL),
                    ln_bias[layer].astype(f32).reshape(1, D_MODEL))
```

```python
import math

import jax
import jax.numpy as jnp
from jax import lax
from jax.experimental import pallas as pl
from jax.experimental.pallas import tpu as pltpu

D_MODEL = 2048
N_META = 16
SSM_WIDTH = 1024
SSM_GROUP = 16
SSM_GROUPS = 64
SSM_STATE = 64
SB_WIDTH = 1024
SB_HEADS = 8
SB_HEAD_DIM = 128
IN_WIDTH = 2 * SSM_WIDTH + 4 * SB_WIDTH
DEEPNORM_ALPHA = 2.0 ** 0.25
LN_EPS = 1e-5

SUBLANES = 8
LANES = 128

FRONT = 1024
PROJ_TM = 1024
PROJ_TN = 1024
Q_SLAB = 2 * SSM_WIDTH // PROJ_TN
SSM_TM = 256
ATT_T = 256
ATT_HEADS = 2
ATT_QBLOCKS = 2
MIX_TM = 512
MIX_TN = 512
META_ROWS = 256

SSM_CHUNKS = 4
CHUNK_CH = SSM_WIDTH // SSM_CHUNKS
CHUNK_ST = SSM_GROUPS * SSM_STATE // SSM_CHUNKS
N_STATE = SSM_GROUPS * SSM_STATE
SEG_LEN = SSM_TM // SUBLANES
SEG_STEPS = (1, 2, 4)
N_POW = SEG_LEN + SUBLANES - 1

EXP2_ZERO_BELOW = -150.0
MASKED_SCORE = -1e30
LOG2_E = 1.0 / math.log(2.0)

VMEM_LIMIT = 56 * 1024 * 1024


def _resident(block_shape, index_map):
    return pl.BlockSpec(block_shape, index_map, pipeline_mode=pl.Buffered(1))


def _cmul(ar, ai, br, bi):
    return ar * br - ai * bi, ar * bi + ai * br


def _ssm_params_kernel(lr_ref, li_ref, logdt_ref, br_ref, bi_ref,
                       bbr_ref, bbi_ref, pwr_ref, pwi_ref):
    lr = lr_ref[...]
    li = li_ref[...]
    dt = jnp.exp(logdt_ref[...])
    mag = jnp.exp(lr * dt)
    ab_r = mag * jnp.cos(li * dt)
    ab_i = mag * jnp.sin(li * dt)
    pr, pi = ab_r, ab_i
    pwr_ref[0] = pr
    pwi_ref[0] = pi
    for e in range(1, SEG_LEN):
        pr, pi = _cmul(pr, pi, ab_r, ab_i)
        pwr_ref[e] = pr
        pwi_ref[e] = pi
    sr, si = pr, pi
    for k in range(1, SUBLANES):
        sr, si = _cmul(sr, si, pr, pi)
        pwr_ref[SEG_LEN - 1 + k] = sr
        pwi_ref[SEG_LEN - 1 + k] = si
    den = lr * lr + li * li
    nr = ab_r - 1.0
    f_r = (nr * lr + ab_i * li) / den
    f_i = (ab_i * lr - nr * li) / den
    br = br_ref[...]
    bi = bi_ref[...]
    bbr_ref[...] = f_r[None] * br - f_i[None] * bi
    bbi_ref[...] = f_r[None] * bi + f_i[None] * br


def _ssm_params(lam_re, lam_im, log_dt, b_re, b_im):
    g, p, c = SSM_GROUPS, SSM_STATE, SSM_GROUP
    f32 = jnp.float32
    return pl.pallas_call(
        _ssm_params_kernel,
        out_shape=(jax.ShapeDtypeStruct((c, g, p), f32),
                   jax.ShapeDtypeStruct((c, g, p), f32),
                   jax.ShapeDtypeStruct((N_POW, g, p), f32),
                   jax.ShapeDtypeStruct((N_POW, g, p), f32)),
        name="ssm_params",
    )(lam_re.astype(f32), lam_im.astype(f32), log_dt.astype(f32).reshape(g, 1),
      jnp.transpose(b_re.astype(f32), (2, 0, 1)), jnp.transpose(b_im.astype(f32), (2, 0, 1)))


def _in_proj_kernel(x_ref, meta_ref, w_ref, o_ref, xb_ref):
    j = pl.program_id(1)
    n = pl.program_id(2)

    @pl.when(jnp.logical_and(n == 0, j == 0))
    def _():
        xb_ref[0:PROJ_TM - META_ROWS, :] = jnp.zeros((PROJ_TM - META_ROWS, D_MODEL), jnp.bfloat16)
        xb_ref[PROJ_TM - META_ROWS:PROJ_TM, :] = meta_ref[...].astype(jnp.bfloat16)

    @pl.when(jnp.logical_and(n == 0, j > 0))
    def _():
        xb_ref[...] = x_ref[0].astype(jnp.bfloat16)

    acc = jnp.dot(xb_ref[...], w_ref[...], preferred_element_type=jnp.float32)
    scale = jnp.where(n == Q_SLAB, LOG2_E / math.sqrt(SB_HEAD_DIM), 1.0).astype(jnp.float32)
    o_ref[0] = (acc * scale).astype(o_ref.dtype)


def _in_proj(x, meta_block, w_in_bf):
    bsz, seq, _ = x.shape
    n_tiles = seq // PROJ_TM + FRONT // PROJ_TM
    lp = seq + FRONT
    return pl.pallas_call(
        _in_proj_kernel,
        grid=(bsz, n_tiles, IN_WIDTH // PROJ_TN),
        in_specs=[
            pl.BlockSpec((1, PROJ_TM, D_MODEL), lambda b, j, n: (b, jnp.maximum(j - 1, 0), 0)),
            _resident((META_ROWS, D_MODEL), lambda b, j, n: (0, 0)),
            pl.BlockSpec((D_MODEL, PROJ_TN), lambda b, j, n: (0, n)),
        ],
        out_specs=pl.BlockSpec((1, PROJ_TM, PROJ_TN), lambda b, j, n: (b, j, n)),
        out_shape=jax.ShapeDtypeStruct((bsz, lp, IN_WIDTH), jnp.bfloat16),
        scratch_shapes=[pltpu.VMEM((PROJ_TM, D_MODEL), jnp.bfloat16)],
        compiler_params=pltpu.CompilerParams(
            dimension_semantics=("arbitrary", "arbitrary", "arbitrary"),
            vmem_limit_bytes=VMEM_LIMIT),
        name="in_proj",
    )(x, meta_block, w_in_bf)


def _gelu_tanh(x):
    return 0.5 * x * (1.0 + jnp.tanh(math.sqrt(2.0 / math.pi) * (x + 0.044715 * (x * x * x))))


def _sigmoid(x):
    return 1.0 / (1.0 + jnp.exp(-x))


def _ssm_kernel(un_ref, u_ref, g_ref, perm_ref, permt_ref, bw_ref, cw_ref, a1r_ref, a1i_ref,
                sgr_ref, sgi_ref, sfr_ref, sfi_ref, fr_ref, fi_ref,
                d_ref, wglu_ref, bglu_ref, o_ref, sa_ref, sb_ref, carry_ref):
    s = pl.program_id(1)
    f32 = jnp.float32
    bf16 = jnp.bfloat16

    @pl.when(s == 0)
    def _():
        carry_ref[...] = jnp.zeros_like(carry_ref)
        sb_ref[...] = jnp.zeros_like(sb_ref)

    def step(sw_ref, sr_ref):
        perm = perm_ref[...]
        un_p = jnp.dot(perm, un_ref[0], preferred_element_type=f32).astype(bf16)
        u_p = jnp.dot(perm, u_ref[0], preferred_element_type=f32).astype(bf16)
        g_p = jnp.dot(perm, g_ref[0], preferred_element_type=f32)

        row = lax.broadcasted_iota(jnp.int32, (SUBLANES, CHUNK_ST), 0)
        ys = []
        for kc in range(SSM_CHUNKS):
            sw_ref[:, kc * 2 * CHUNK_ST:(kc + 1) * 2 * CHUNK_ST] = jnp.dot(
                un_p[:, kc * CHUNK_CH:(kc + 1) * CHUNK_CH], bw_ref[kc], preferred_element_type=f32)
            re0 = kc * 2 * CHUNK_ST
            im0 = re0 + CHUNK_ST
            cm0 = kc * CHUNK_ST
            last = SSM_TM - SUBLANES
            er = sr_ref[last:SSM_TM, re0:re0 + CHUNK_ST]
            ei = sr_ref[last:SSM_TM, im0:im0 + CHUNK_ST]
            for k, d in enumerate(SEG_STEPS):
                mr = sgr_ref[k, :, cm0:cm0 + CHUNK_ST]
                mi = sgi_ref[k, :, cm0:cm0 + CHUNK_ST]
                pr = pltpu.roll(er, d, 0)
                pi = pltpu.roll(ei, d, 0)
                er, ei = er + (mr * pr - mi * pi), ei + (mr * pi + mi * pr)
            c_r = carry_ref[:, re0:re0 + CHUNK_ST]
            c_i = carry_ref[:, im0:im0 + CHUNK_ST]
            sfr = sfr_ref[:, cm0:cm0 + CHUNK_ST]
            sfi = sfi_ref[:, cm0:cm0 + CHUNK_ST]
            hr = er + (sfr * c_r - sfi * c_i)
            hi = ei + (sfr * c_i + sfi * c_r)
            in_r = jnp.where(row == 0, c_r, pltpu.roll(hr, 1, 0))
            in_i = jnp.where(row == 0, c_i, pltpu.roll(hi, 1, 0))
            carry_ref[:, re0:re0 + CHUNK_ST] = jnp.broadcast_to(hr[SUBLANES - 1:SUBLANES, :],
                                                                 (SUBLANES, CHUNK_ST))
            carry_ref[:, im0:im0 + CHUNK_ST] = jnp.broadcast_to(hi[SUBLANES - 1:SUBLANES, :],
                                                                 (SUBLANES, CHUNK_ST))
            tin_r = jnp.concatenate([in_r] * SEG_LEN, axis=0)
            tin_i = jnp.concatenate([in_i] * SEG_LEN, axis=0)
            fr = fr_ref[:, cm0:cm0 + CHUNK_ST]
            fi = fi_ref[:, cm0:cm0 + CHUNK_ST]
            h_r = sr_ref[:, re0:re0 + CHUNK_ST] + (fr * tin_r - fi * tin_i)
            h_i = sr_ref[:, im0:im0 + CHUNK_ST] + (fr * tin_i + fi * tin_r)
            h_bf = jnp.concatenate([h_r.astype(bf16), h_i.astype(bf16)], axis=1)
            ys.append(jnp.dot(h_bf, cw_ref[kc], preferred_element_type=f32))
        y = jnp.concatenate(ys, axis=1) + d_ref[...] * u_p.astype(f32)
        z = _gelu_tanh(y).astype(bf16)
        hg = jnp.dot(z, wglu_ref[...], preferred_element_type=f32) + bglu_ref[...]
        glu = hg[:, :SSM_WIDTH] * _sigmoid(hg[:, SSM_WIDTH:])
        out_p = (glu * (g_p * _sigmoid(g_p))).astype(bf16)
        o_ref[0] = jnp.dot(permt_ref[...], out_p, preferred_element_type=f32).astype(o_ref.dtype)

        lane_tiles = CHUNK_ST // LANES
        for kc in range(SSM_CHUNKS):
            cols = [(kc * 2 * CHUNK_ST + c * LANES,
                     kc * 2 * CHUNK_ST + CHUNK_ST + c * LANES,
                     kc * CHUNK_ST + c * LANES) for c in range(lane_tiles)]

            def body(t, state, cols=cols):
                r0 = pl.multiple_of(t * SUBLANES, SUBLANES)
                new = []
                for idx, (cr, ci, cm) in enumerate(cols):
                    ar = a1r_ref[:, cm:cm + LANES]
                    ai = a1i_ref[:, cm:cm + LANES]
                    xr = state[2 * idx]
                    xi = state[2 * idx + 1]
                    nr = (ar * xr - ai * xi) + sw_ref[pl.ds(r0, SUBLANES), cr:cr + LANES]
                    ni = (ar * xi + ai * xr) + sw_ref[pl.ds(r0, SUBLANES), ci:ci + LANES]
                    sw_ref[pl.ds(r0, SUBLANES), cr:cr + LANES] = nr
                    sw_ref[pl.ds(r0, SUBLANES), ci:ci + LANES] = ni
                    new.append(nr)
                    new.append(ni)
                return tuple(new)

            zero = jnp.zeros((SUBLANES, LANES), f32)
            lax.fori_loop(0, SEG_LEN, body, tuple(zero for _ in range(2 * lane_tiles)))

    @pl.when(s % 2 == 0)
    def _():
        step(sa_ref, sb_ref)

    @pl.when(s % 2 == 1)
    def _():
        step(sb_ref, sa_ref)


def _ssm(proj, perm, perm_t, bw, cw, tables, d_row, w_glu_bf, b_glu_row, seq):
    bsz = proj.shape[0]
    t0 = FRONT // SSM_TM - 1
    n_tiles = seq // SSM_TM + 1
    consts = (perm, perm_t, bw, cw) + tuple(tables) + (d_row, w_glu_bf, b_glu_row)

    def const_spec(a):
        nd = a.ndim
        return _resident(a.shape, lambda b, s, nd=nd: (0,) * nd)

    def cur(s):
        return jnp.maximum(s - 1, 0)

    return pl.pallas_call(
        _ssm_kernel,
        grid=(bsz, n_tiles + 1),
        in_specs=[
            pl.BlockSpec((1, SSM_TM, SSM_WIDTH), lambda b, s: (b, t0 + jnp.minimum(s, n_tiles - 1), 0)),
            pl.BlockSpec((1, SSM_TM, SSM_WIDTH), lambda b, s: (b, t0 + cur(s), 0)),
            pl.BlockSpec((1, SSM_TM, SSM_WIDTH), lambda b, s: (b, t0 + cur(s), 1)),
        ] + [const_spec(a) for a in consts],
        out_specs=pl.BlockSpec((1, SSM_TM, SSM_WIDTH), lambda b, s: (b, jnp.maximum(s - 2, 0), 0)),
        out_shape=jax.ShapeDtypeStruct((bsz, seq, SSM_WIDTH), jnp.bfloat16),
        scratch_shapes=[pltpu.VMEM((SSM_TM, 2 * N_STATE), jnp.float32),
                        pltpu.VMEM((SSM_TM, 2 * N_STATE), jnp.float32),
                        pltpu.VMEM((SUBLANES, 2 * N_STATE), jnp.float32)],
        compiler_params=pltpu.CompilerParams(
            dimension_semantics=("arbitrary", "arbitrary"),
            vmem_limit_bytes=VMEM_LIMIT),
        name="ssm",
    )(proj, proj, proj, *consts)


def _attention_kernel(q_ref, k_ref, v_ref, g_ref, tri_ref, o_ref, acc_ref, run_ref):
    qi = pl.program_id(2)
    t = ATT_T
    hd = SB_HEAD_DIM
    f32 = jnp.float32
    bf16 = jnp.bfloat16
    qs = [FRONT + (qi * ATT_QBLOCKS + c) * t for c in range(ATT_QBLOCKS)]
    n_blocks = (qi + 1) * ATT_QBLOCKS + 1
    row = lax.broadcasted_iota(jnp.int32, (t, t), 0)
    col = lax.broadcasted_iota(jnp.int32, (t, t), 1)

    def block(c, h, ks, run, causal):
        hs = slice(h * hd, (h + 1) * hd)
        q = q_ref[0, c * t:(c + 1) * t, hs]
        kb = k_ref[0, pl.ds(ks, t), hs]
        vb = v_ref[0, pl.ds(ks, t), hs]
        z = lax.dot_general(q, kb, (((1,), (1,)), ((), ())), preferred_element_type=f32)
        if causal:
            z = jnp.where(col < row, z, MASKED_SCORE)
        sp = jnp.maximum(z, 0.0) + jnp.log2(1.0 + jnp.exp2(-jnp.abs(z)))
        hi = sp.astype(bf16)
        lo = (sp - hi.astype(f32)).astype(bf16)
        nsum = jnp.dot(jnp.concatenate([hi, lo], axis=1), tri_ref[...],
                       preferred_element_type=f32)
        a = jnp.exp2(z + nsum + jnp.concatenate([run] * (t // LANES), axis=1))
        pv = jnp.dot(a.astype(bf16), vb, preferred_element_type=f32)
        return pv, run + jnp.broadcast_to(nsum[:, 0:1], (t, LANES))

    def pair(c, h, ks, run, first):
        pv0, run = block(c, h, pl.multiple_of(ks, t), run, first)
        pv1, run = block(c, h, pl.multiple_of(ks - t, t), run, False)
        return pv0 + pv1, run

    live = jnp.bool_(False)
    for c in range(ATT_QBLOCKS):
        rows = slice(c * t, (c + 1) * t)
        for h in range(ATT_HEADS):
            lanes = slice(h * LANES, (h + 1) * LANES)
            pv, run = pair(c, h, qs[c], jnp.zeros((t, LANES), f32), True)
            acc_ref[rows, lanes] = pv
            run_ref[rows, lanes] = run
            live = jnp.logical_or(live, jnp.max(run) >= EXP2_ZERO_BELOW)

    def cond(state):
        j, live = state
        return jnp.logical_and(j < n_blocks, live)

    def body(state):
        j, _ = state
        live = jnp.bool_(False)
        for c in range(ATT_QBLOCKS):
            rows = slice(c * t, (c + 1) * t)
            for h in range(ATT_HEADS):
                lanes = slice(h * LANES, (h + 1) * LANES)
                pv, run = pair(c, h, qs[c] - j * t, run_ref[rows, lanes], False)
                acc_ref[rows, lanes] += pv
                run_ref[rows, lanes] = run
                live = jnp.logical_or(live, jnp.max(run) >= EXP2_ZERO_BELOW)
        return j + 2, live

    lax.while_loop(cond, body, (jnp.int32(2), live))
    g = g_ref[0].astype(f32)
    o_ref[0] = (acc_ref[...] * (g * _sigmoid(g))).astype(o_ref.dtype)


def _attention(proj, tri, seq):
    bsz, lp, _ = proj.shape
    t = ATT_T
    w = ATT_HEADS * SB_HEAD_DIM
    q_col0 = 2 * SSM_WIDTH // w
    k_col0 = q_col0 + SB_WIDTH // w
    v_col0 = k_col0 + SB_WIDTH // w
    g_col0 = v_col0 + SB_WIDTH // w
    tq = ATT_QBLOCKS * t
    assert FRONT % tq == 0 and FRONT >= (ATT_QBLOCKS + 1) * t
    return pl.pallas_call(
        _attention_kernel,
        grid=(bsz, SB_HEADS // ATT_HEADS, seq // tq),
        in_specs=[
            pl.BlockSpec((1, tq, w), lambda b, h, i: (b, FRONT // tq + i, q_col0 + h)),
            pl.BlockSpec((1, lp, w), lambda b, h, i: (b, 0, k_col0 + h)),
            pl.BlockSpec((1, lp, w), lambda b, h, i: (b, 0, v_col0 + h)),
            pl.BlockSpec((1, tq, w), lambda b, h, i: (b, FRONT // tq + i, g_col0 + h)),
            _resident(tri.shape, lambda b, h, i: (0, 0)),
        ],
        out_specs=pl.BlockSpec((1, tq, w), lambda b, h, i: (b, i, h)),
        out_shape=jax.ShapeDtypeStruct((bsz, seq, SB_WIDTH), jnp.bfloat16),
        scratch_shapes=[pltpu.VMEM((tq, w), jnp.float32),
                        pltpu.VMEM((tq, ATT_HEADS * LANES), jnp.float32)],
        compiler_params=pltpu.CompilerParams(
            dimension_semantics=("arbitrary", "arbitrary", "arbitrary"),
            vmem_limit_bytes=VMEM_LIMIT),
        name="attention",
    )(proj, proj, proj, proj, tri)


def _mix_out_kernel(x_ref, a_ref, s_ref, wa_ref, ws_ref, wg1_ref, wg2_ref, bg1_ref, bg2_ref,
                    wo_ref, gain_ref, bias_ref, o_ref, xb_ref, acc_ref):
    n = pl.program_id(2)

    @pl.when(n == 0)
    def _():
        xb_ref[...] = x_ref[0].astype(jnp.bfloat16)
        acc_ref[...] = jnp.zeros_like(acc_ref)

    f32 = jnp.float32
    xb = xb_ref[...]
    y_ssm = jnp.dot(a_ref[0], wa_ref[...], preferred_element_type=f32)
    y_sb = jnp.dot(s_ref[0], ws_ref[...], preferred_element_type=f32)
    g1 = _sigmoid(jnp.dot(xb, wg1_ref[...], preferred_element_type=f32) + bg1_ref[...])
    g2 = _sigmoid(jnp.dot(xb, wg2_ref[...], preferred_element_type=f32) + bg2_ref[...])
    mixed = (g1 * y_ssm + g2 * y_sb).astype(jnp.bfloat16)
    acc_ref[...] += jnp.dot(mixed, wo_ref[...], preferred_element_type=f32)

    @pl.when(n == pl.num_programs(2) - 1)
    def _():
        r = DEEPNORM_ALPHA * x_ref[0] + acc_ref[...]
        mu = jnp.mean(r, axis=-1, keepdims=True)
        c = r - mu
        var = jnp.mean(c * c, axis=-1, keepdims=True)
        o_ref[0] = c * lax.rsqrt(var + LN_EPS) * gain_ref[...] + bias_ref[...]


def _mix_out(x, a_ssm, a_sb, wa, ws, wg, bg_row, wo, gain_row, bias_row):
    bsz, seq, _ = x.shape
    n_col = D_MODEL // MIX_TN
    return pl.pallas_call(
        _mix_out_kernel,
        grid=(bsz, seq // MIX_TM, n_col),
        in_specs=[
            pl.BlockSpec((1, MIX_TM, D_MODEL), lambda b, j, n: (b, j, 0)),
            pl.BlockSpec((1, MIX_TM, SSM_WIDTH), lambda b, j, n: (b, j, 0)),
            pl.BlockSpec((1, MIX_TM, SB_WIDTH), lambda b, j, n: (b, j, 0)),
            pl.BlockSpec((SSM_WIDTH, MIX_TN), lambda b, j, n: (0, n)),
            pl.BlockSpec((SB_WIDTH, MIX_TN), lambda b, j, n: (0, n)),
            pl.BlockSpec((D_MODEL, MIX_TN), lambda b, j, n: (0, n)),
            pl.BlockSpec((D_MODEL, MIX_TN), lambda b, j, n: (0, n_col + n)),
            pl.BlockSpec((1, MIX_TN), lambda b, j, n: (0, n)),
            pl.BlockSpec((1, MIX_TN), lambda b, j, n: (0, n_col + n)),
            pl.BlockSpec((MIX_TN, D_MODEL), lambda b, j, n: (n, 0)),
            _resident((1, D_MODEL), lambda b, j, n: (0, 0)),
            _resident((1, D_MODEL), lambda b, j, n: (0, 0)),
        ],
        out_specs=pl.BlockSpec((1, MIX_TM, D_MODEL), lambda b, j, n: (b, j, 0)),
        out_shape=jax.ShapeDtypeStruct((bsz, seq, D_MODEL), jnp.float32),
        scratch_shapes=[pltpu.VMEM((MIX_TM, D_MODEL), jnp.bfloat16),
                        pltpu.VMEM((MIX_TM, D_MODEL), jnp.float32)],
        compiler_params=pltpu.CompilerParams(
            dimension_semantics=("arbitrary", "arbitrary", "arbitrary"),
            vmem_limit_bytes=VMEM_LIMIT),
        name="mix_out",
    )(x, a_ssm, a_sb, wa, ws, wg, wg, bg_row, bg_row, wo, gain_row, bias_row)


def _block_diag_b(bb_r, bb_i):
    gl = SSM_GROUPS // SSM_CHUNKS
    eye = jnp.eye(gl, dtype=jnp.float32)

    def part(bb):
        b4 = jnp.transpose(bb, (1, 0, 2)).reshape(SSM_CHUNKS, gl, SSM_GROUP, SSM_STATE)
        return jnp.einsum('kgcp,gh->kgchp', b4, eye).reshape(SSM_CHUNKS, CHUNK_CH, CHUNK_ST)

    return jnp.concatenate([part(bb_r), part(bb_i)], axis=2).astype(jnp.bfloat16)


def _block_diag_c(c_re, c_im):
    gl = SSM_GROUPS // SSM_CHUNKS
    eye = jnp.eye(gl, dtype=jnp.float32)

    def part(c):
        c4 = c.astype(jnp.float32).reshape(SSM_CHUNKS, gl, SSM_GROUP, SSM_STATE)
        return jnp.einsum('kgcp,gh->kgphc', c4, eye).reshape(SSM_CHUNKS, CHUNK_ST, CHUNK_CH)

    return jnp.concatenate([part(c_re), -part(c_im)], axis=1).astype(jnp.bfloat16)


def _scan_tables(pw_r, pw_i):
    rows = jnp.arange(SUBLANES)[:, None]

    def tables(pw):
        p = pw.reshape(N_POW, N_STATE)
        a1 = jnp.broadcast_to(p[0][None, :], (SUBLANES, N_STATE))
        seg = p[SEG_LEN - 1:]
        sg = jnp.stack([jnp.where(rows >= d, seg[d - 1][None, :], 0.0) for d in SEG_STEPS])
        f = jnp.repeat(p[:SEG_LEN], SUBLANES, axis=0)
        return a1, sg, seg, f

    a1r, sgr, sfr, fr = tables(pw_r)
    a1i, sgi, sfi, fi = tables(pw_i)
    return a1r, a1i, sgr, sgi, sfr, sfi, fr, fi


def _scan_permutation():
    r = jnp.arange(SSM_TM)
    tok = (r % SUBLANES) * SEG_LEN + r // SUBLANES
    perm = (tok[:, None] == jnp.arange(SSM_TM)[None, :]).astype(jnp.bfloat16)
    return perm, perm.T


def kernel(x, meta_tokens, w_in, ssm_lambda_re, ssm_lambda_im, ssm_log_dt, ssm_b_re, ssm_b_im,
           ssm_c_re, ssm_c_im, ssm_d, w_glu, b_glu, w_branch_ssm, w_branch_sb, w_gate, b_gate,
           w_out, ln_gain, ln_bias):
    bsz, seq, _ = x.shape
    f32 = jnp.float32
    bf16 = jnp.bfloat16
    layer = 0

    bb_r, bb_i, pw_r, pw_i = _ssm_params(ssm_lambda_re[layer], ssm_lambda_im[layer],
                                         ssm_log_dt[layer], ssm_b_re[layer], ssm_b_im[layer])
    bw = _block_diag_b(bb_r, bb_i)
    cw = _block_diag_c(ssm_c_re[layer], ssm_c_im[layer])
    tables = _scan_tables(pw_r, pw_i)
    perm, perm_t = _scan_permutation()

    meta_block = jnp.concatenate(
        [jnp.zeros((META_ROWS - N_META, D_MODEL), f32), meta_tokens.astype(f32)], axis=0)
    proj = _in_proj(x.astype(f32), meta_block, w_in[layer].astype(bf16))

    a_ssm = _ssm(proj, perm, perm_t, bw, cw, tables,
                 ssm_d[layer].astype(f32).reshape(1, SSM_WIDTH),
                 w_glu[layer].astype(bf16), b_glu[layer].astype(f32).reshape(1, 2 * SSM_WIDTH), seq)

    t = ATT_T
    r = jnp.arange(t)
    neg_lower = -(r[:, None] >= r[None, :]).astype(bf16)
    tri = jnp.concatenate([neg_lower, neg_lower], axis=0)
    a_sb = _attention(proj, tri, seq)

    return _mix_out(x.astype(f32), a_ssm, a_sb,
                    w_branch_ssm[layer].astype(bf16), w_branch_sb[layer].astype(bf16),
                    w_gate[layer].astype(bf16), b_gate[layer].astype(f32).reshape(1, 2 * D_MODEL),
                    w_out[layer].astype(bf16),
                    ln_gain[layer].astype(f32).reshape(1, D_MODEL),
                    ln_bias[layer].astype(f32).reshape(1, D_MODEL))
```

```python
import math

import jax
import jax.numpy as jnp
from jax import lax
from jax.experimental import pallas as pl
from jax.experimental.pallas import tpu as pltpu

D_MODEL = 2048
N_META = 16
SSM_WIDTH = 1024
SSM_GROUP = 16
SSM_GROUPS = 64
SSM_STATE = 64
SB_WIDTH = 1024
SB_HEADS = 8
SB_HEAD_DIM = 128
IN_WIDTH = 2 * SSM_WIDTH + 4 * SB_WIDTH
DEEPNORM_ALPHA = 2.0 ** 0.25
LN_EPS = 1e-5

SUBLANES = 8
LANES = 128

FRONT = 1024
PROJ_TM = 1024
PROJ_TN = 2048
Q_COL0 = 2 * SSM_WIDTH
SSM_TM = 256
ATT_T = 256
ATT_HEADS = 2
SLAB = ATT_HEADS * SB_HEAD_DIM
ATT_QBLOCKS = 2
MIX_TM = 512
MIX_TN = 512
META_ROWS = 256

SSM_CHUNKS = 4
CHUNK_CH = SSM_WIDTH // SSM_CHUNKS
CHUNK_ST = SSM_GROUPS * SSM_STATE // SSM_CHUNKS
N_STATE = SSM_GROUPS * SSM_STATE
SEG_LEN = SSM_TM // SUBLANES
SEG_STEPS = (1, 2, 4)
N_POW = SEG_LEN + SUBLANES - 1

EXP2_ZERO_BELOW = -150.0
MASKED_SCORE = -1e30
LOG2_E = 1.0 / math.log(2.0)

VMEM_LIMIT = 56 * 1024 * 1024


def _resident(block_shape, index_map):
    return pl.BlockSpec(block_shape, index_map, pipeline_mode=pl.Buffered(1))


def _cmul(ar, ai, br, bi):
    return ar * br - ai * bi, ar * bi + ai * br


def _ssm_params_kernel(lr_ref, li_ref, logdt_ref, br_ref, bi_ref,
                       bbr_ref, bbi_ref, pwr_ref, pwi_ref):
    lr = lr_ref[...]
    li = li_ref[...]
    dt = jnp.exp(logdt_ref[...])
    mag = jnp.exp(lr * dt)
    ab_r = mag * jnp.cos(li * dt)
    ab_i = mag * jnp.sin(li * dt)
    pr, pi = ab_r, ab_i
    pwr_ref[0] = pr
    pwi_ref[0] = pi
    for e in range(1, SEG_LEN):
        pr, pi = _cmul(pr, pi, ab_r, ab_i)
        pwr_ref[e] = pr
        pwi_ref[e] = pi
    sr, si = pr, pi
    for k in range(1, SUBLANES):
        sr, si = _cmul(sr, si, pr, pi)
        pwr_ref[SEG_LEN - 1 + k] = sr
        pwi_ref[SEG_LEN - 1 + k] = si
    den = lr * lr + li * li
    nr = ab_r - 1.0
    f_r = (nr * lr + ab_i * li) / den
    f_i = (ab_i * lr - nr * li) / den
    br = br_ref[...]
    bi = bi_ref[...]
    bbr_ref[...] = f_r[None] * br - f_i[None] * bi
    bbi_ref[...] = f_r[None] * bi + f_i[None] * br


def _ssm_params(lam_re, lam_im, log_dt, b_re, b_im):
    g, p, c = SSM_GROUPS, SSM_STATE, SSM_GROUP
    f32 = jnp.float32
    return pl.pallas_call(
        _ssm_params_kernel,
        out_shape=(jax.ShapeDtypeStruct((c, g, p), f32),
                   jax.ShapeDtypeStruct((c, g, p), f32),
                   jax.ShapeDtypeStruct((N_POW, g, p), f32),
                   jax.ShapeDtypeStruct((N_POW, g, p), f32)),
        name="ssm_params",
    )(lam_re.astype(f32), lam_im.astype(f32), log_dt.astype(f32).reshape(g, 1),
      jnp.transpose(b_re.astype(f32), (2, 0, 1)), jnp.transpose(b_im.astype(f32), (2, 0, 1)))


def _in_proj_kernel(x_ref, meta_ref, w_ref, o_ref, xb_ref):
    j = pl.program_id(1)
    n = pl.program_id(2)

    @pl.when(jnp.logical_and(n == 0, j == 0))
    def _():
        xb_ref[0:PROJ_TM - META_ROWS, :] = jnp.zeros((PROJ_TM - META_ROWS, D_MODEL), jnp.bfloat16)
        xb_ref[PROJ_TM - META_ROWS:PROJ_TM, :] = meta_ref[...].astype(jnp.bfloat16)

    @pl.when(jnp.logical_and(n == 0, j > 0))
    def _():
        xb_ref[...] = x_ref[0].astype(jnp.bfloat16)

    acc = jnp.dot(xb_ref[...], w_ref[...], preferred_element_type=jnp.float32)
    col = n * PROJ_TN + lax.broadcasted_iota(jnp.int32, (1, PROJ_TN), 1)
    is_q = jnp.logical_and(col >= Q_COL0, col < Q_COL0 + SB_WIDTH)
    scale = jnp.where(is_q, LOG2_E / math.sqrt(SB_HEAD_DIM), 1.0).astype(jnp.float32)
    res = (acc * scale).astype(o_ref.dtype)
    for c in range(PROJ_TN // SLAB):
        o_ref[0, c] = res[:, c * SLAB:(c + 1) * SLAB]


def _in_proj(x, meta_block, w_in_bf):
    bsz, seq, _ = x.shape
    n_tiles = seq // PROJ_TM + FRONT // PROJ_TM
    lp = seq + FRONT
    return pl.pallas_call(
        _in_proj_kernel,
        grid=(bsz, n_tiles, IN_WIDTH // PROJ_TN),
        in_specs=[
            pl.BlockSpec((1, PROJ_TM, D_MODEL), lambda b, j, n: (b, jnp.maximum(j - 1, 0), 0)),
            _resident((META_ROWS, D_MODEL), lambda b, j, n: (0, 0)),
            pl.BlockSpec((D_MODEL, PROJ_TN), lambda b, j, n: (0, n)),
        ],
        out_specs=pl.BlockSpec((1, PROJ_TN // SLAB, PROJ_TM, SLAB), lambda b, j, n: (b, n, j, 0)),
        out_shape=jax.ShapeDtypeStruct((bsz, IN_WIDTH // SLAB, lp, SLAB), jnp.bfloat16),
        scratch_shapes=[pltpu.VMEM((PROJ_TM, D_MODEL), jnp.bfloat16)],
        compiler_params=pltpu.CompilerParams(
            dimension_semantics=("arbitrary", "arbitrary", "arbitrary"),
            vmem_limit_bytes=VMEM_LIMIT),
        name="in_proj",
    )(x, meta_block, w_in_bf)


def _gelu_tanh(x):
    return 0.5 * x * (1.0 + jnp.tanh(math.sqrt(2.0 / math.pi) * (x + 0.044715 * (x * x * x))))


def _sigmoid(x):
    return 1.0 / (1.0 + jnp.exp(-x))


def _ssm_kernel(un_ref, u_ref, g_ref, perm_ref, permt_ref, bw_ref, cw_ref, a1r_ref, a1i_ref,
                sgr_ref, sgi_ref, sfr_ref, sfi_ref, fr_ref, fi_ref,
                d_ref, wglu_ref, bglu_ref, o_ref, sa_ref, sb_ref, carry_ref):
    s = pl.program_id(1)
    f32 = jnp.float32
    bf16 = jnp.bfloat16

    @pl.when(s == 0)
    def _():
        carry_ref[...] = jnp.zeros_like(carry_ref)
        sb_ref[...] = jnp.zeros_like(sb_ref)

    def step(sw_ref, sr_ref):
        perm = perm_ref[...]

        def permuted(ref):
            return jnp.concatenate([jnp.dot(perm, ref[0, c], preferred_element_type=f32)
                                    for c in range(ref.shape[1])], axis=1)

        un_p = permuted(un_ref).astype(bf16)
        u_p = permuted(u_ref).astype(bf16)
        g_p = permuted(g_ref)

        row = lax.broadcasted_iota(jnp.int32, (SUBLANES, CHUNK_ST), 0)
        ys = []
        for kc in range(SSM_CHUNKS):
            sw_ref[:, kc * 2 * CHUNK_ST:(kc + 1) * 2 * CHUNK_ST] = jnp.dot(
                un_p[:, kc * CHUNK_CH:(kc + 1) * CHUNK_CH], bw_ref[kc], preferred_element_type=f32)
            re0 = kc * 2 * CHUNK_ST
            im0 = re0 + CHUNK_ST
            cm0 = kc * CHUNK_ST
            last = SSM_TM - SUBLANES
            er = sr_ref[last:SSM_TM, re0:re0 + CHUNK_ST]
            ei = sr_ref[last:SSM_TM, im0:im0 + CHUNK_ST]
            for k, d in enumerate(SEG_STEPS):
                mr = sgr_ref[k, :, cm0:cm0 + CHUNK_ST]
                mi = sgi_ref[k, :, cm0:cm0 + CHUNK_ST]
                pr = pltpu.roll(er, d, 0)
                pi = pltpu.roll(ei, d, 0)
                er, ei = er + (mr * pr - mi * pi), ei + (mr * pi + mi * pr)
            c_r = carry_ref[:, re0:re0 + CHUNK_ST]
            c_i = carry_ref[:, im0:im0 + CHUNK_ST]
            sfr = sfr_ref[:, cm0:cm0 + CHUNK_ST]
            sfi = sfi_ref[:, cm0:cm0 + CHUNK_ST]
            hr = er + (sfr * c_r - sfi * c_i)
            hi = ei + (sfr * c_i + sfi * c_r)
            in_r = jnp.where(row == 0, c_r, pltpu.roll(hr, 1, 0))
            in_i = jnp.where(row == 0, c_i, pltpu.roll(hi, 1, 0))
            carry_ref[:, re0:re0 + CHUNK_ST] = jnp.broadcast_to(hr[SUBLANES - 1:SUBLANES, :],
                                                                 (SUBLANES, CHUNK_ST))
            carry_ref[:, im0:im0 + CHUNK_ST] = jnp.broadcast_to(hi[SUBLANES - 1:SUBLANES, :],
                                                                 (SUBLANES, CHUNK_ST))
            tin_r = jnp.concatenate([in_r] * SEG_LEN, axis=0)
            tin_i = jnp.concatenate([in_i] * SEG_LEN, axis=0)
            fr = fr_ref[:, cm0:cm0 + CHUNK_ST]
            fi = fi_ref[:, cm0:cm0 + CHUNK_ST]
            h_r = sr_ref[:, re0:re0 + CHUNK_ST] + (fr * tin_r - fi * tin_i)
            h_i = sr_ref[:, im0:im0 + CHUNK_ST] + (fr * tin_i + fi * tin_r)
            h_bf = jnp.concatenate([h_r.astype(bf16), h_i.astype(bf16)], axis=1)
            ys.append(jnp.dot(h_bf, cw_ref[kc], preferred_element_type=f32))
        y = jnp.concatenate(ys, axis=1) + d_ref[...] * u_p.astype(f32)
        z = _gelu_tanh(y).astype(bf16)
        hg = jnp.dot(z, wglu_ref[...], preferred_element_type=f32) + bglu_ref[...]
        glu = hg[:, :SSM_WIDTH] * _sigmoid(hg[:, SSM_WIDTH:])
        out_p = (glu * (g_p * _sigmoid(g_p))).astype(bf16)
        o_ref[0] = jnp.dot(permt_ref[...], out_p, preferred_element_type=f32).astype(o_ref.dtype)

        lane_tiles = CHUNK_ST // LANES
        for kc in range(SSM_CHUNKS):
            cols = [(kc * 2 * CHUNK_ST + c * LANES,
                     kc * 2 * CHUNK_ST + CHUNK_ST + c * LANES,
                     kc * CHUNK_ST + c * LANES) for c in range(lane_tiles)]

            def body(t, state, cols=cols):
                r0 = pl.multiple_of(t * SUBLANES, SUBLANES)
                new = []
                for idx, (cr, ci, cm) in enumerate(cols):
                    ar = a1r_ref[:, cm:cm + LANES]
                    ai = a1i_ref[:, cm:cm + LANES]
                    xr = state[2 * idx]
                    xi = state[2 * idx + 1]
                    nr = (ar * xr - ai * xi) + sw_ref[pl.ds(r0, SUBLANES), cr:cr + LANES]
                    ni = (ar * xi + ai * xr) + sw_ref[pl.ds(r0, SUBLANES), ci:ci + LANES]
                    sw_ref[pl.ds(r0, SUBLANES), cr:cr + LANES] = nr
                    sw_ref[pl.ds(r0, SUBLANES), ci:ci + LANES] = ni
                    new.append(nr)
                    new.append(ni)
                return tuple(new)

            zero = jnp.zeros((SUBLANES, LANES), f32)
            lax.fori_loop(0, SEG_LEN, body, tuple(zero for _ in range(2 * lane_tiles)))

    @pl.when(s % 2 == 0)
    def _():
        step(sa_ref, sb_ref)

    @pl.when(s % 2 == 1)
    def _():
        step(sb_ref, sa_ref)


def _ssm(proj, perm, perm_t, bw, cw, tables, d_row, w_glu_bf, b_glu_row, seq):
    bsz = proj.shape[0]
    t0 = FRONT // SSM_TM - 1
    n_tiles = seq // SSM_TM + 1
    consts = (perm, perm_t, bw, cw) + tuple(tables) + (d_row, w_glu_bf, b_glu_row)

    def const_spec(a):
        nd = a.ndim
        return _resident(a.shape, lambda b, s, nd=nd: (0,) * nd)

    def cur(s):
        return jnp.maximum(s - 1, 0)

    ns = SSM_WIDTH // SLAB

    return pl.pallas_call(
        _ssm_kernel,
        grid=(bsz, n_tiles + 1),
        in_specs=[
            pl.BlockSpec((1, ns, SSM_TM, SLAB), lambda b, s: (b, 0, t0 + jnp.minimum(s, n_tiles - 1), 0)),
            pl.BlockSpec((1, ns, SSM_TM, SLAB), lambda b, s: (b, 0, t0 + cur(s), 0)),
            pl.BlockSpec((1, ns, SSM_TM, SLAB), lambda b, s: (b, 1, t0 + cur(s), 0)),
        ] + [const_spec(a) for a in consts],
        out_specs=pl.BlockSpec((1, SSM_TM, SSM_WIDTH), lambda b, s: (b, jnp.maximum(s - 2, 0), 0)),
        out_shape=jax.ShapeDtypeStruct((bsz, seq, SSM_WIDTH), jnp.bfloat16),
        scratch_shapes=[pltpu.VMEM((SSM_TM, 2 * N_STATE), jnp.float32),
                        pltpu.VMEM((SSM_TM, 2 * N_STATE), jnp.float32),
                        pltpu.VMEM((SUBLANES, 2 * N_STATE), jnp.float32)],
        compiler_params=pltpu.CompilerParams(
            dimension_semantics=("arbitrary", "arbitrary"),
            vmem_limit_bytes=VMEM_LIMIT),
        name="ssm",
    )(proj, proj, proj, *consts)


def _attention_kernel(q_ref, k_ref, v_ref, g_ref, tri_ref, o_ref, acc_ref, run_ref):
    qi = pl.program_id(2)
    t = ATT_T
    hd = SB_HEAD_DIM
    f32 = jnp.float32
    bf16 = jnp.bfloat16
    qs = [FRONT + (qi * ATT_QBLOCKS + c) * t for c in range(ATT_QBLOCKS)]
    n_blocks = (qi + 1) * ATT_QBLOCKS + 1
    row = lax.broadcasted_iota(jnp.int32, (t, t), 0)
    col = lax.broadcasted_iota(jnp.int32, (t, t), 1)

    def block(c, h, ks, run, causal):
        hs = slice(h * hd, (h + 1) * hd)
        q = q_ref[0, 0, c * t:(c + 1) * t, hs]
        kb = k_ref[0, 0, pl.ds(ks, t), hs]
        vb = v_ref[0, 0, pl.ds(ks, t), hs]
        z = lax.dot_general(q, kb, (((1,), (1,)), ((), ())), preferred_element_type=f32)
        if causal:
            z = jnp.where(col < row, z, MASKED_SCORE)
        sp = jnp.maximum(z, 0.0) + jnp.log2(1.0 + jnp.exp2(-jnp.abs(z)))
        hi = sp.astype(bf16)
        lo = (sp - hi.astype(f32)).astype(bf16)
        nsum = jnp.dot(jnp.concatenate([hi, lo], axis=1), tri_ref[...],
                       preferred_element_type=f32)
        a = jnp.exp2(z + nsum + jnp.concatenate([run] * (t // LANES), axis=1))
        pv = jnp.dot(a.astype(bf16), vb, preferred_element_type=f32)
        return pv, run + jnp.broadcast_to(nsum[:, 0:1], (t, LANES))

    def pair(c, h, ks, run, first):
        pv0, run = block(c, h, pl.multiple_of(ks, t), run, first)
        pv1, run = block(c, h, pl.multiple_of(ks - t, t), run, False)
        return pv0 + pv1, run

    live = jnp.bool_(False)
    for c in range(ATT_QBLOCKS):
        rows = slice(c * t, (c + 1) * t)
        for h in range(ATT_HEADS):
            lanes = slice(h * LANES, (h + 1) * LANES)
            pv, run = pair(c, h, qs[c], jnp.zeros((t, LANES), f32), True)
            acc_ref[rows, lanes] = pv
            run_ref[rows, lanes] = run
            live = jnp.logical_or(live, jnp.max(run) >= EXP2_ZERO_BELOW)

    def cond(state):
        j, live = state
        return jnp.logical_and(j < n_blocks, live)

    def body(state):
        j, _ = state
        live = jnp.bool_(False)
        for c in range(ATT_QBLOCKS):
            rows = slice(c * t, (c + 1) * t)
            for h in range(ATT_HEADS):
                lanes = slice(h * LANES, (h + 1) * LANES)
                pv, run = pair(c, h, qs[c] - j * t, run_ref[rows, lanes], False)
                acc_ref[rows, lanes] += pv
                run_ref[rows, lanes] = run
                live = jnp.logical_or(live, jnp.max(run) >= EXP2_ZERO_BELOW)
        return j + 2, live

    lax.while_loop(cond, body, (jnp.int32(2), live))
    g = g_ref[0, 0].astype(f32)
    o_ref[0] = (acc_ref[...] * (g * _sigmoid(g))).astype(o_ref.dtype)


def _attention(proj, tri, seq):
    bsz, _, lp, w = proj.shape
    t = ATT_T
    q_col0 = 2 * SSM_WIDTH // w
    k_col0 = q_col0 + SB_WIDTH // w
    v_col0 = k_col0 + SB_WIDTH // w
    g_col0 = v_col0 + SB_WIDTH // w
    tq = ATT_QBLOCKS * t
    assert FRONT % tq == 0 and FRONT >= (ATT_QBLOCKS + 1) * t
    return pl.pallas_call(
        _attention_kernel,
        grid=(bsz, SB_HEADS // ATT_HEADS, seq // tq),
        in_specs=[
            pl.BlockSpec((1, 1, tq, w), lambda b, h, i: (b, q_col0 + h, FRONT // tq + i, 0)),
            pl.BlockSpec((1, 1, lp, w), lambda b, h, i: (b, k_col0 + h, 0, 0)),
            pl.BlockSpec((1, 1, lp, w), lambda b, h, i: (b, v_col0 + h, 0, 0)),
            pl.BlockSpec((1, 1, tq, w), lambda b, h, i: (b, g_col0 + h, FRONT // tq + i, 0)),
            _resident(tri.shape, lambda b, h, i: (0, 0)),
        ],
        out_specs=pl.BlockSpec((1, tq, w), lambda b, h, i: (b, i, h)),
        out_shape=jax.ShapeDtypeStruct((bsz, seq, SB_WIDTH), jnp.bfloat16),
        scratch_shapes=[pltpu.VMEM((tq, w), jnp.float32),
                        pltpu.VMEM((tq, ATT_HEADS * LANES), jnp.float32)],
        compiler_params=pltpu.CompilerParams(
            dimension_semantics=("arbitrary", "arbitrary", "arbitrary"),
            vmem_limit_bytes=VMEM_LIMIT),
        name="attention",
    )(proj, proj, proj, proj, tri)


def _mix_out_kernel(x_ref, a_ref, s_ref, wa_ref, ws_ref, wg1_ref, wg2_ref, bg1_ref, bg2_ref,
                    wo_ref, gain_ref, bias_ref, o_ref, xb_ref, acc_ref):
    n = pl.program_id(2)

    @pl.when(n == 0)
    def _():
        x = x_ref[0]
        xb_ref[...] = x.astype(jnp.bfloat16)
        acc_ref[...] = DEEPNORM_ALPHA * x

    f32 = jnp.float32
    xb = xb_ref[...]
    y_ssm = jnp.dot(a_ref[0], wa_ref[...], preferred_element_type=f32)
    y_sb = jnp.dot(s_ref[0], ws_ref[...], preferred_element_type=f32)
    g1 = _sigmoid(jnp.dot(xb, wg1_ref[...], preferred_element_type=f32) + bg1_ref[...])
    g2 = _sigmoid(jnp.dot(xb, wg2_ref[...], preferred_element_type=f32) + bg2_ref[...])
    mixed = (g1 * y_ssm + g2 * y_sb).astype(jnp.bfloat16)
    acc_ref[...] += jnp.dot(mixed, wo_ref[...], preferred_element_type=f32)

    @pl.when(n == pl.num_programs(2) - 1)
    def _():
        r = acc_ref[...]
        mu = jnp.mean(r, axis=-1, keepdims=True)
        c = r - mu
        var = jnp.mean(c * c, axis=-1, keepdims=True)
        o_ref[0] = c * lax.rsqrt(var + LN_EPS) * gain_ref[...] + bias_ref[...]


def _mix_out(x, a_ssm, a_sb, wa, ws, wg, bg_row, wo, gain_row, bias_row):
    bsz, seq, _ = x.shape
    n_col = D_MODEL // MIX_TN
    return pl.pallas_call(
        _mix_out_kernel,
        grid=(bsz, seq // MIX_TM, n_col),
        in_specs=[
            pl.BlockSpec((1, MIX_TM, D_MODEL), lambda b, j, n: (b, j, 0)),
            pl.BlockSpec((1, MIX_TM, SSM_WIDTH), lambda b, j, n: (b, j, 0)),
            pl.BlockSpec((1, MIX_TM, SB_WIDTH), lambda b, j, n: (b, j, 0)),
            pl.BlockSpec((SSM_WIDTH, MIX_TN), lambda b, j, n: (0, n)),
            pl.BlockSpec((SB_WIDTH, MIX_TN), lambda b, j, n: (0, n)),
            pl.BlockSpec((D_MODEL, MIX_TN), lambda b, j, n: (0, n)),
            pl.BlockSpec((D_MODEL, MIX_TN), lambda b, j, n: (0, n_col + n)),
            pl.BlockSpec((1, MIX_TN), lambda b, j, n: (0, n)),
            pl.BlockSpec((1, MIX_TN), lambda b, j, n: (0, n_col + n)),
            pl.BlockSpec((MIX_TN, D_MODEL), lambda b, j, n: (n, 0)),
            _resident((1, D_MODEL), lambda b, j, n: (0, 0)),
            _resident((1, D_MODEL), lambda b, j, n: (0, 0)),
        ],
        out_specs=pl.BlockSpec((1, MIX_TM, D_MODEL), lambda b, j, n: (b, j, 0)),
        out_shape=jax.ShapeDtypeStruct((bsz, seq, D_MODEL), jnp.float32),
        scratch_shapes=[pltpu.VMEM((MIX_TM, D_MODEL), jnp.bfloat16),
                        pltpu.VMEM((MIX_TM, D_MODEL), jnp.float32)],
        compiler_params=pltpu.CompilerParams(
            dimension_semantics=("arbitrary", "arbitrary", "arbitrary"),
            vmem_limit_bytes=VMEM_LIMIT),
        name="mix_out",
    )(x, a_ssm, a_sb, wa, ws, wg, wg, bg_row, bg_row, wo, gain_row, bias_row)


def _block_diag_b(bb_r, bb_i):
    gl = SSM_GROUPS // SSM_CHUNKS
    eye = jnp.eye(gl, dtype=jnp.float32)

    def part(bb):
        b4 = jnp.transpose(bb, (1, 0, 2)).reshape(SSM_CHUNKS, gl, SSM_GROUP, SSM_STATE)
        return jnp.einsum('kgcp,gh->kgchp', b4, eye).reshape(SSM_CHUNKS, CHUNK_CH, CHUNK_ST)

    return jnp.concatenate([part(bb_r), part(bb_i)], axis=2).astype(jnp.bfloat16)


def _block_diag_c(c_re, c_im):
    gl = SSM_GROUPS // SSM_CHUNKS
    eye = jnp.eye(gl, dtype=jnp.float32)

    def part(c):
        c4 = c.astype(jnp.float32).reshape(SSM_CHUNKS, gl, SSM_GROUP, SSM_STATE)
        return jnp.einsum('kgcp,gh->kgphc', c4, eye).reshape(SSM_CHUNKS, CHUNK_ST, CHUNK_CH)

    return jnp.concatenate([part(c_re), -part(c_im)], axis=1).astype(jnp.bfloat16)


def _scan_tables(pw_r, pw_i):
    rows = jnp.arange(SUBLANES)[:, None]

    def tables(pw):
        p = pw.reshape(N_POW, N_STATE)
        a1 = jnp.broadcast_to(p[0][None, :], (SUBLANES, N_STATE))
        seg = p[SEG_LEN - 1:]
        sg = jnp.stack([jnp.where(rows >= d, seg[d - 1][None, :], 0.0) for d in SEG_STEPS])
        f = jnp.repeat(p[:SEG_LEN], SUBLANES, axis=0)
        return a1, sg, seg, f

    a1r, sgr, sfr, fr = tables(pw_r)
    a1i, sgi, sfi, fi = tables(pw_i)
    return a1r, a1i, sgr, sgi, sfr, sfi, fr, fi


def _scan_permutation():
    r = jnp.arange(SSM_TM)
    tok = (r % SUBLANES) * SEG_LEN + r // SUBLANES
    perm = (tok[:, None] == jnp.arange(SSM_TM)[None, :]).astype(jnp.bfloat16)
    return perm, perm.T


def kernel(x, meta_tokens, w_in, ssm_lambda_re, ssm_lambda_im, ssm_log_dt, ssm_b_re, ssm_b_im,
           ssm_c_re, ssm_c_im, ssm_d, w_glu, b_glu, w_branch_ssm, w_branch_sb, w_gate, b_gate,
           w_out, ln_gain, ln_bias):
    bsz, seq, _ = x.shape
    f32 = jnp.float32
    bf16 = jnp.bfloat16
    layer = 0

    bb_r, bb_i, pw_r, pw_i = _ssm_params(ssm_lambda_re[layer], ssm_lambda_im[layer],
                                         ssm_log_dt[layer], ssm_b_re[layer], ssm_b_im[layer])
    bw = _block_diag_b(bb_r, bb_i)
    cw = _block_diag_c(ssm_c_re[layer], ssm_c_im[layer])
    tables = _scan_tables(pw_r, pw_i)
    perm, perm_t = _scan_permutation()

    meta_block = jnp.concatenate(
        [jnp.zeros((META_ROWS - N_META, D_MODEL), f32), meta_tokens.astype(f32)], axis=0)
    proj = _in_proj(x.astype(f32), meta_block, w_in[layer].astype(bf16))

    a_ssm = _ssm(proj, perm, perm_t, bw, cw, tables,
                 ssm_d[layer].astype(f32).reshape(1, SSM_WIDTH),
                 w_glu[layer].astype(bf16), b_glu[layer].astype(f32).reshape(1, 2 * SSM_WIDTH), seq)

    t = ATT_T
    r = jnp.arange(t)
    neg_lower = -(r[:, None] >= r[None, :]).astype(bf16)
    tri = jnp.concatenate([neg_lower, neg_lower], axis=0)
    a_sb = _attention(proj, tri, seq)

    return _mix_out(x.astype(f32), a_ssm, a_sb,
                    w_branch_ssm[layer].astype(bf16), w_branch_sb[layer].astype(bf16),
                    w_gate[layer].astype(bf16), b_gate[layer].astype(f32).reshape(1, 2 * D_MODEL),
                    w_out[layer].astype(bf16),
                    ln_gain[layer].astype(f32).reshape(1, D_MODEL),
                    ln_bias[layer].astype(f32).reshape(1, D_MODEL))
```

```python
import math

import jax
import jax.numpy as jnp
from jax import lax
from jax.experimental import pallas as pl
from jax.experimental.pallas import tpu as pltpu

D_MODEL = 2048
N_META = 16
SSM_WIDTH = 1024
SSM_GROUP = 16
SSM_GROUPS = 64
SSM_STATE = 64
SB_WIDTH = 1024
SB_HEADS = 8
SB_HEAD_DIM = 128
IN_WIDTH = 2 * SSM_WIDTH + 4 * SB_WIDTH
DEEPNORM_ALPHA = 2.0 ** 0.25
LN_EPS = 1e-5

SUBLANES = 8
LANES = 128

FRONT = 1024
PROJ_TM = 1024
PROJ_TN = 2048
Q_COL0 = 2 * SSM_WIDTH
SSM_TM = 256
ATT_T = 256
ATT_HEADS = 2
SLAB = ATT_HEADS * SB_HEAD_DIM
ATT_QBLOCKS = 4
MIX_TM = 256
META_ROWS = 256

SSM_CHUNKS = 4
CHUNK_CH = SSM_WIDTH // SSM_CHUNKS
CHUNK_ST = SSM_GROUPS * SSM_STATE // SSM_CHUNKS
N_STATE = SSM_GROUPS * SSM_STATE
SEG_LEN = SSM_TM // SUBLANES
SEG_STEPS = (1, 2, 4)
N_POW = SEG_LEN + SUBLANES - 1

EXP2_ZERO_BELOW = -150.0
MASKED_SCORE = -1e30
LOG2_E = 1.0 / math.log(2.0)

VMEM_LIMIT = 56 * 1024 * 1024


def _resident(block_shape, index_map):
    return pl.BlockSpec(block_shape, index_map, pipeline_mode=pl.Buffered(1))


def _cmul(ar, ai, br, bi):
    return ar * br - ai * bi, ar * bi + ai * br


def _ssm_params_kernel(lr_ref, li_ref, logdt_ref, br_ref, bi_ref,
                       bbr_ref, bbi_ref, pwr_ref, pwi_ref):
    lr = lr_ref[...]
    li = li_ref[...]
    dt = jnp.exp(logdt_ref[...])
    mag = jnp.exp(lr * dt)
    ab_r = mag * jnp.cos(li * dt)
    ab_i = mag * jnp.sin(li * dt)
    pr, pi = ab_r, ab_i
    pwr_ref[0] = pr
    pwi_ref[0] = pi
    for e in range(1, SEG_LEN):
        pr, pi = _cmul(pr, pi, ab_r, ab_i)
        pwr_ref[e] = pr
        pwi_ref[e] = pi
    sr, si = pr, pi
    for k in range(1, SUBLANES):
        sr, si = _cmul(sr, si, pr, pi)
        pwr_ref[SEG_LEN - 1 + k] = sr
        pwi_ref[SEG_LEN - 1 + k] = si
    den = lr * lr + li * li
    nr = ab_r - 1.0
    f_r = (nr * lr + ab_i * li) / den
    f_i = (ab_i * lr - nr * li) / den
    br = br_ref[...]
    bi = bi_ref[...]
    bbr_ref[...] = f_r[None] * br - f_i[None] * bi
    bbi_ref[...] = f_r[None] * bi + f_i[None] * br


def _ssm_params(lam_re, lam_im, log_dt, b_re, b_im):
    g, p, c = SSM_GROUPS, SSM_STATE, SSM_GROUP
    f32 = jnp.float32
    return pl.pallas_call(
        _ssm_params_kernel,
        out_shape=(jax.ShapeDtypeStruct((c, g, p), f32),
                   jax.ShapeDtypeStruct((c, g, p), f32),
                   jax.ShapeDtypeStruct((N_POW, g, p), f32),
                   jax.ShapeDtypeStruct((N_POW, g, p), f32)),
        name="ssm_params",
    )(lam_re.astype(f32), lam_im.astype(f32), log_dt.astype(f32).reshape(g, 1),
      jnp.transpose(b_re.astype(f32), (2, 0, 1)), jnp.transpose(b_im.astype(f32), (2, 0, 1)))


def _in_proj_kernel(x_ref, meta_ref, w_ref, o_ref, xb_ref):
    j = pl.program_id(1)
    n = pl.program_id(2)

    @pl.when(jnp.logical_and(n == 0, j == 0))
    def _():
        xb_ref[0:PROJ_TM - META_ROWS, :] = jnp.zeros((PROJ_TM - META_ROWS, D_MODEL), jnp.bfloat16)
        xb_ref[PROJ_TM - META_ROWS:PROJ_TM, :] = meta_ref[...].astype(jnp.bfloat16)

    @pl.when(jnp.logical_and(n == 0, j > 0))
    def _():
        xb_ref[...] = x_ref[0].astype(jnp.bfloat16)

    acc = jnp.dot(xb_ref[...], w_ref[...], preferred_element_type=jnp.float32)
    col = n * PROJ_TN + lax.broadcasted_iota(jnp.int32, (1, PROJ_TN), 1)
    is_q = jnp.logical_and(col >= Q_COL0, col < Q_COL0 + SB_WIDTH)
    scale = jnp.where(is_q, LOG2_E / math.sqrt(SB_HEAD_DIM), 1.0).astype(jnp.float32)
    res = (acc * scale).astype(o_ref.dtype)
    for c in range(PROJ_TN // SLAB):
        o_ref[0, c] = res[:, c * SLAB:(c + 1) * SLAB]


def _in_proj(x, meta_block, w_in_bf):
    bsz, seq, _ = x.shape
    n_tiles = seq // PROJ_TM + FRONT // PROJ_TM
    lp = seq + FRONT
    return pl.pallas_call(
        _in_proj_kernel,
        grid=(bsz, n_tiles, IN_WIDTH // PROJ_TN),
        in_specs=[
            pl.BlockSpec((1, PROJ_TM, D_MODEL), lambda b, j, n: (b, jnp.maximum(j - 1, 0), 0)),
            _resident((META_ROWS, D_MODEL), lambda b, j, n: (0, 0)),
            pl.BlockSpec((D_MODEL, PROJ_TN), lambda b, j, n: (0, n)),
        ],
        out_specs=pl.BlockSpec((1, PROJ_TN // SLAB, PROJ_TM, SLAB), lambda b, j, n: (b, n, j, 0)),
        out_shape=jax.ShapeDtypeStruct((bsz, IN_WIDTH // SLAB, lp, SLAB), jnp.bfloat16),
        scratch_shapes=[pltpu.VMEM((PROJ_TM, D_MODEL), jnp.bfloat16)],
        compiler_params=pltpu.CompilerParams(
            dimension_semantics=("arbitrary", "arbitrary", "arbitrary"),
            vmem_limit_bytes=VMEM_LIMIT),
        name="in_proj",
    )(x, meta_block, w_in_bf)


def _gelu_tanh(x):
    return 0.5 * x * (1.0 + jnp.tanh(math.sqrt(2.0 / math.pi) * (x + 0.044715 * (x * x * x))))


def _sigmoid(x):
    return 1.0 / (1.0 + jnp.exp(-x))


def _ssm_kernel(un_ref, u_ref, g_ref, perm_ref, permt_ref, bw_ref, cw_ref, a1r_ref, a1i_ref,
                sgr_ref, sgi_ref, sfr_ref, sfi_ref, fr_ref, fi_ref,
                d_ref, wglu_ref, bglu_ref, o_ref, sa_ref, sb_ref, carry_ref):
    s = pl.program_id(1)
    f32 = jnp.float32
    bf16 = jnp.bfloat16

    @pl.when(s == 0)
    def _():
        carry_ref[...] = jnp.zeros_like(carry_ref)
        sb_ref[...] = jnp.zeros_like(sb_ref)

    def step(sw_ref, sr_ref):
        perm = perm_ref[...]

        def permuted(ref):
            return jnp.concatenate([jnp.dot(perm, ref[0, c], preferred_element_type=f32)
                                    for c in range(ref.shape[1])], axis=1)

        un_p = permuted(un_ref).astype(bf16)
        u_p = permuted(u_ref).astype(bf16)
        g_p = permuted(g_ref)

        row = lax.broadcasted_iota(jnp.int32, (SUBLANES, CHUNK_ST), 0)
        ys = []
        for kc in range(SSM_CHUNKS):
            sw_ref[:, kc * 2 * CHUNK_ST:(kc + 1) * 2 * CHUNK_ST] = jnp.dot(
                un_p[:, kc * CHUNK_CH:(kc + 1) * CHUNK_CH], bw_ref[kc], preferred_element_type=f32)
            re0 = kc * 2 * CHUNK_ST
            im0 = re0 + CHUNK_ST
            cm0 = kc * CHUNK_ST
            last = SSM_TM - SUBLANES
            er = sr_ref[last:SSM_TM, re0:re0 + CHUNK_ST]
            ei = sr_ref[last:SSM_TM, im0:im0 + CHUNK_ST]
            for k, d in enumerate(SEG_STEPS):
                mr = sgr_ref[k, :, cm0:cm0 + CHUNK_ST]
                mi = sgi_ref[k, :, cm0:cm0 + CHUNK_ST]
                pr = pltpu.roll(er, d, 0)
                pi = pltpu.roll(ei, d, 0)
                er, ei = er + (mr * pr - mi * pi), ei + (mr * pi + mi * pr)
            c_r = carry_ref[:, re0:re0 + CHUNK_ST]
            c_i = carry_ref[:, im0:im0 + CHUNK_ST]
            sfr = sfr_ref[:, cm0:cm0 + CHUNK_ST]
            sfi = sfi_ref[:, cm0:cm0 + CHUNK_ST]
            hr = er + (sfr * c_r - sfi * c_i)
            hi = ei + (sfr * c_i + sfi * c_r)
            in_r = jnp.where(row == 0, c_r, pltpu.roll(hr, 1, 0))
            in_i = jnp.where(row == 0, c_i, pltpu.roll(hi, 1, 0))
            carry_ref[:, re0:re0 + CHUNK_ST] = jnp.broadcast_to(hr[SUBLANES - 1:SUBLANES, :],
                                                                 (SUBLANES, CHUNK_ST))
            carry_ref[:, im0:im0 + CHUNK_ST] = jnp.broadcast_to(hi[SUBLANES - 1:SUBLANES, :],
                                                                 (SUBLANES, CHUNK_ST))
            tin_r = jnp.concatenate([in_r] * SEG_LEN, axis=0)
            tin_i = jnp.concatenate([in_i] * SEG_LEN, axis=0)
            fr = fr_ref[:, cm0:cm0 + CHUNK_ST]
            fi = fi_ref[:, cm0:cm0 + CHUNK_ST]
            h_r = sr_ref[:, re0:re0 + CHUNK_ST] + (fr * tin_r - fi * tin_i)
            h_i = sr_ref[:, im0:im0 + CHUNK_ST] + (fr * tin_i + fi * tin_r)
            h_bf = jnp.concatenate([h_r.astype(bf16), h_i.astype(bf16)], axis=1)
            ys.append(jnp.dot(h_bf, cw_ref[kc], preferred_element_type=f32))
        y = jnp.concatenate(ys, axis=1) + d_ref[...] * u_p.astype(f32)
        z = _gelu_tanh(y).astype(bf16)
        hg = jnp.dot(z, wglu_ref[...], preferred_element_type=f32) + bglu_ref[...]
        glu = hg[:, :SSM_WIDTH] * _sigmoid(hg[:, SSM_WIDTH:])
        out_p = (glu * (g_p * _sigmoid(g_p))).astype(bf16)
        o_ref[0] = jnp.dot(permt_ref[...], out_p, preferred_element_type=f32).astype(o_ref.dtype)

        lane_tiles = CHUNK_ST // LANES
        for kc in range(SSM_CHUNKS):
            cols = [(kc * 2 * CHUNK_ST + c * LANES,
                     kc * 2 * CHUNK_ST + CHUNK_ST + c * LANES,
                     kc * CHUNK_ST + c * LANES) for c in range(lane_tiles)]

            def body(t, state, cols=cols):
                r0 = pl.multiple_of(t * SUBLANES, SUBLANES)
                new = []
                for idx, (cr, ci, cm) in enumerate(cols):
                    ar = a1r_ref[:, cm:cm + LANES]
                    ai = a1i_ref[:, cm:cm + LANES]
                    xr = state[2 * idx]
                    xi = state[2 * idx + 1]
                    nr = (ar * xr - ai * xi) + sw_ref[pl.ds(r0, SUBLANES), cr:cr + LANES]
                    ni = (ar * xi + ai * xr) + sw_ref[pl.ds(r0, SUBLANES), ci:ci + LANES]
                    sw_ref[pl.ds(r0, SUBLANES), cr:cr + LANES] = nr
                    sw_ref[pl.ds(r0, SUBLANES), ci:ci + LANES] = ni
                    new.append(nr)
                    new.append(ni)
                return tuple(new)

            zero = jnp.zeros((SUBLANES, LANES), f32)
            lax.fori_loop(0, SEG_LEN, body, tuple(zero for _ in range(2 * lane_tiles)))

    @pl.when(s % 2 == 0)
    def _():
        step(sa_ref, sb_ref)

    @pl.when(s % 2 == 1)
    def _():
        step(sb_ref, sa_ref)


def _ssm(proj, perm, perm_t, bw, cw, tables, d_row, w_glu_bf, b_glu_row, seq):
    bsz = proj.shape[0]
    t0 = FRONT // SSM_TM - 1
    n_tiles = seq // SSM_TM + 1
    consts = (perm, perm_t, bw, cw) + tuple(tables) + (d_row, w_glu_bf, b_glu_row)

    def const_spec(a):
        nd = a.ndim
        return _resident(a.shape, lambda b, s, nd=nd: (0,) * nd)

    def cur(s):
        return jnp.maximum(s - 1, 0)

    ns = SSM_WIDTH // SLAB

    return pl.pallas_call(
        _ssm_kernel,
        grid=(bsz, n_tiles + 1),
        in_specs=[
            pl.BlockSpec((1, ns, SSM_TM, SLAB), lambda b, s: (b, 0, t0 + jnp.minimum(s, n_tiles - 1), 0)),
            pl.BlockSpec((1, ns, SSM_TM, SLAB), lambda b, s: (b, 0, t0 + cur(s), 0)),
            pl.BlockSpec((1, ns, SSM_TM, SLAB), lambda b, s: (b, 1, t0 + cur(s), 0)),
        ] + [const_spec(a) for a in consts],
        out_specs=pl.BlockSpec((1, SSM_TM, SSM_WIDTH), lambda b, s: (b, jnp.maximum(s - 2, 0), 0)),
        out_shape=jax.ShapeDtypeStruct((bsz, seq, SSM_WIDTH), jnp.bfloat16),
        scratch_shapes=[pltpu.VMEM((SSM_TM, 2 * N_STATE), jnp.float32),
                        pltpu.VMEM((SSM_TM, 2 * N_STATE), jnp.float32),
                        pltpu.VMEM((SUBLANES, 2 * N_STATE), jnp.float32)],
        compiler_params=pltpu.CompilerParams(
            dimension_semantics=("arbitrary", "arbitrary"),
            vmem_limit_bytes=VMEM_LIMIT),
        name="ssm",
    )(proj, proj, proj, *consts)


def _attention_kernel(q_ref, k_ref, v_ref, g_ref, tri_ref, o_ref, acc_ref, run_ref):
    qi = pl.program_id(2)
    t = ATT_T
    hd = SB_HEAD_DIM
    f32 = jnp.float32
    bf16 = jnp.bfloat16
    qs = [FRONT + (qi * ATT_QBLOCKS + c) * t for c in range(ATT_QBLOCKS)]
    n_blocks = (qi + 1) * ATT_QBLOCKS + 1
    row = lax.broadcasted_iota(jnp.int32, (t, t), 0)
    col = lax.broadcasted_iota(jnp.int32, (t, t), 1)

    def block(c, h, ks, run, causal):
        hs = slice(h * hd, (h + 1) * hd)
        q = q_ref[0, 0, c * t:(c + 1) * t, hs]
        kb = k_ref[0, 0, pl.ds(ks, t), hs]
        vb = v_ref[0, 0, pl.ds(ks, t), hs]
        z = lax.dot_general(q, kb, (((1,), (1,)), ((), ())), preferred_element_type=f32)
        if causal:
            z = jnp.where(col < row, z, MASKED_SCORE)
        sp = jnp.maximum(z, 0.0) + jnp.log2(1.0 + jnp.exp2(-jnp.abs(z)))
        hi = sp.astype(bf16)
        lo = (sp - hi.astype(f32)).astype(bf16)
        nsum = jnp.dot(jnp.concatenate([hi, lo], axis=1), tri_ref[...],
                       preferred_element_type=f32)
        a = jnp.exp2(z + nsum + jnp.concatenate([run] * (t // LANES), axis=1))
        pv = jnp.dot(a.astype(bf16), vb, preferred_element_type=f32)
        return pv, run + jnp.broadcast_to(nsum[:, 0:1], (t, LANES))

    def pair(c, h, ks, run, first):
        pv0, run = block(c, h, pl.multiple_of(jnp.maximum(ks, 0), t), run, first)
        pv1, run = block(c, h, pl.multiple_of(jnp.maximum(ks - t, 0), t), run, False)
        return pv0 + pv1, run

    live = jnp.bool_(False)
    for c in range(ATT_QBLOCKS):
        rows = slice(c * t, (c + 1) * t)
        for h in range(ATT_HEADS):
            lanes = slice(h * LANES, (h + 1) * LANES)
            pv, run = pair(c, h, qs[c], jnp.zeros((t, LANES), f32), True)
            acc_ref[rows, lanes] = pv
            run_ref[rows, lanes] = run
            live = jnp.logical_or(live, jnp.max(run) >= EXP2_ZERO_BELOW)

    def cond(state):
        j, live = state
        return jnp.logical_and(j < n_blocks, live)

    def body(state):
        j, _ = state
        live = jnp.bool_(False)
        for c in range(ATT_QBLOCKS):
            rows = slice(c * t, (c + 1) * t)
            for h in range(ATT_HEADS):
                lanes = slice(h * LANES, (h + 1) * LANES)
                pv, run = pair(c, h, qs[c] - j * t, run_ref[rows, lanes], False)
                acc_ref[rows, lanes] += pv
                run_ref[rows, lanes] = run
                live = jnp.logical_or(live, jnp.max(run) >= EXP2_ZERO_BELOW)
        return j + 2, live

    lax.while_loop(cond, body, (jnp.int32(2), live))
    g = g_ref[0, 0].astype(f32)
    o_ref[0] = (acc_ref[...] * (g * _sigmoid(g))).astype(o_ref.dtype)


def _attention(proj, tri, seq):
    bsz, _, lp, w = proj.shape
    t = ATT_T
    q_col0 = 2 * SSM_WIDTH // w
    k_col0 = q_col0 + SB_WIDTH // w
    v_col0 = k_col0 + SB_WIDTH // w
    g_col0 = v_col0 + SB_WIDTH // w
    tq = ATT_QBLOCKS * t
    assert FRONT % tq == 0 and FRONT >= 2 * t
    return pl.pallas_call(
        _attention_kernel,
        grid=(bsz, SB_HEADS // ATT_HEADS, seq // tq),
        in_specs=[
            pl.BlockSpec((1, 1, tq, w), lambda b, h, i: (b, q_col0 + h, FRONT // tq + i, 0)),
            pl.BlockSpec((1, 1, lp, w), lambda b, h, i: (b, k_col0 + h, 0, 0)),
            pl.BlockSpec((1, 1, lp, w), lambda b, h, i: (b, v_col0 + h, 0, 0)),
            pl.BlockSpec((1, 1, tq, w), lambda b, h, i: (b, g_col0 + h, FRONT // tq + i, 0)),
            _resident(tri.shape, lambda b, h, i: (0, 0)),
        ],
        out_specs=pl.BlockSpec((1, tq, w), lambda b, h, i: (b, i, h)),
        out_shape=jax.ShapeDtypeStruct((bsz, seq, SB_WIDTH), jnp.bfloat16),
        scratch_shapes=[pltpu.VMEM((tq, w), jnp.float32),
                        pltpu.VMEM((tq, ATT_HEADS * LANES), jnp.float32)],
        compiler_params=pltpu.CompilerParams(
            dimension_semantics=("arbitrary", "arbitrary", "arbitrary"),
            vmem_limit_bytes=VMEM_LIMIT),
        name="attention",
    )(proj, proj, proj, proj, tri)


def _mix_out_kernel(x_ref, a_ref, s_ref, wa_ref, ws_ref, wg_ref, bg_ref,
                    wo_ref, gain_ref, bias_ref, o_ref):
    f32 = jnp.float32
    x = x_ref[0]
    xb = x.astype(jnp.bfloat16)
    a = a_ref[0]
    s = s_ref[0]
    y_ssm = jnp.dot(a, wa_ref[...], preferred_element_type=f32)
    y_sb = jnp.dot(s, ws_ref[...], preferred_element_type=f32)
    gates = _sigmoid(jnp.dot(xb, wg_ref[...], preferred_element_type=f32) + bg_ref[...])
    mixed = (gates[:, :D_MODEL] * y_ssm + gates[:, D_MODEL:] * y_sb).astype(jnp.bfloat16)
    r = DEEPNORM_ALPHA * x + jnp.dot(mixed, wo_ref[...], preferred_element_type=f32)
    mu = jnp.mean(r, axis=-1, keepdims=True)
    c = r - mu
    var = jnp.mean(c * c, axis=-1, keepdims=True)
    o_ref[0] = c * lax.rsqrt(var + LN_EPS) * gain_ref[...] + bias_ref[...]


def _mix_out(x, a_ssm, a_sb, wa, ws, wg, bg_row, wo, gain_row, bias_row):
    bsz, seq, _ = x.shape
    consts = (wa, ws, wg, bg_row, wo, gain_row, bias_row)
    return pl.pallas_call(
        _mix_out_kernel,
        grid=(bsz, seq // MIX_TM),
        in_specs=[
            pl.BlockSpec((1, MIX_TM, D_MODEL), lambda b, j: (b, j, 0)),
            pl.BlockSpec((1, MIX_TM, SSM_WIDTH), lambda b, j: (b, j, 0)),
            pl.BlockSpec((1, MIX_TM, SB_WIDTH), lambda b, j: (b, j, 0)),
        ] + [_resident(w.shape, lambda b, j: (0, 0)) for w in consts],
        out_specs=pl.BlockSpec((1, MIX_TM, D_MODEL), lambda b, j: (b, j, 0)),
        out_shape=jax.ShapeDtypeStruct((bsz, seq, D_MODEL), jnp.float32),
        compiler_params=pltpu.CompilerParams(
            dimension_semantics=("arbitrary", "arbitrary"),
            vmem_limit_bytes=VMEM_LIMIT),
        name="mix_out",
    )(x, a_ssm, a_sb, *consts)


def _block_diag_b(bb_r, bb_i):
    gl = SSM_GROUPS // SSM_CHUNKS
    eye = jnp.eye(gl, dtype=jnp.float32)

    def part(bb):
        b4 = jnp.transpose(bb, (1, 0, 2)).reshape(SSM_CHUNKS, gl, SSM_GROUP, SSM_STATE)
        return jnp.einsum('kgcp,gh->kgchp', b4, eye).reshape(SSM_CHUNKS, CHUNK_CH, CHUNK_ST)

    return jnp.concatenate([part(bb_r), part(bb_i)], axis=2).astype(jnp.bfloat16)


def _block_diag_c(c_re, c_im):
    gl = SSM_GROUPS // SSM_CHUNKS
    eye = jnp.eye(gl, dtype=jnp.float32)

    def part(c):
        c4 = c.astype(jnp.float32).reshape(SSM_CHUNKS, gl, SSM_GROUP, SSM_STATE)
        return jnp.einsum('kgcp,gh->kgphc', c4, eye).reshape(SSM_CHUNKS, CHUNK_ST, CHUNK_CH)

    return jnp.concatenate([part(c_re), -part(c_im)], axis=1).astype(jnp.bfloat16)


def _scan_tables(pw_r, pw_i):
    rows = jnp.arange(SUBLANES)[:, None]

    def tables(pw):
        p = pw.reshape(N_POW, N_STATE)
        a1 = jnp.broadcast_to(p[0][None, :], (SUBLANES, N_STATE))
        seg = p[SEG_LEN - 1:]
        sg = jnp.stack([jnp.where(rows >= d, seg[d - 1][None, :], 0.0) for d in SEG_STEPS])
        f = jnp.repeat(p[:SEG_LEN], SUBLANES, axis=0)
        return a1, sg, seg, f

    a1r, sgr, sfr, fr = tables(pw_r)
    a1i, sgi, sfi, fi = tables(pw_i)
    return a1r, a1i, sgr, sgi, sfr, sfi, fr, fi


def _scan_permutation():
    r = jnp.arange(SSM_TM)
    tok = (r % SUBLANES) * SEG_LEN + r // SUBLANES
    perm = (tok[:, None] == jnp.arange(SSM_TM)[None, :]).astype(jnp.bfloat16)
    return perm, perm.T


def kernel(x, meta_tokens, w_in, ssm_lambda_re, ssm_lambda_im, ssm_log_dt, ssm_b_re, ssm_b_im,
           ssm_c_re, ssm_c_im, ssm_d, w_glu, b_glu, w_branch_ssm, w_branch_sb, w_gate, b_gate,
           w_out, ln_gain, ln_bias):
    bsz, seq, _ = x.shape
    f32 = jnp.float32
    bf16 = jnp.bfloat16
    layer = 0

    bb_r, bb_i, pw_r, pw_i = _ssm_params(ssm_lambda_re[layer], ssm_lambda_im[layer],
                                         ssm_log_dt[layer], ssm_b_re[layer], ssm_b_im[layer])
    bw = _block_diag_b(bb_r, bb_i)
    cw = _block_diag_c(ssm_c_re[layer], ssm_c_im[layer])
    tables = _scan_tables(pw_r, pw_i)
    perm, perm_t = _scan_permutation()

    meta_block = jnp.concatenate(
        [jnp.zeros((META_ROWS - N_META, D_MODEL), f32), meta_tokens.astype(f32)], axis=0)
    proj = _in_proj(x.astype(f32), meta_block, w_in[layer].astype(bf16))

    a_ssm = _ssm(proj, perm, perm_t, bw, cw, tables,
                 ssm_d[layer].astype(f32).reshape(1, SSM_WIDTH),
                 w_glu[layer].astype(bf16), b_glu[layer].astype(f32).reshape(1, 2 * SSM_WIDTH), seq)

    t = ATT_T
    r = jnp.arange(t)
    neg_lower = -(r[:, None] >= r[None, :]).astype(bf16)
    tri = jnp.concatenate([neg_lower, neg_lower], axis=0)
    a_sb = _attention(proj, tri, seq)

    return _mix_out(x.astype(f32), a_ssm, a_sb,
                    w_branch_ssm[layer].astype(bf16), w_branch_sb[layer].astype(bf16),
                    w_gate[layer].astype(bf16), b_gate[layer].astype(f32).reshape(1, 2 * D_MODEL),
                    w_out[layer].astype(bf16),
                    ln_gain[layer].astype(f32).reshape(1, D_MODEL),
                    ln_bias[layer].astype(f32).reshape(1, D_MODEL))
```

```python
import math

import jax
import jax.numpy as jnp
import numpy as np
from jax import lax
from jax.experimental import pallas as pl
from jax.experimental.pallas import tpu as pltpu

D_MODEL = 2048
N_META = 16
SSM_WIDTH = 1024
SSM_GROUP = 16
SSM_GROUPS = 64
SSM_STATE = 64
SB_WIDTH = 1024
SB_HEADS = 8
SB_HEAD_DIM = 128
IN_WIDTH = 2 * SSM_WIDTH + 4 * SB_WIDTH
DEEPNORM_ALPHA = 2.0 ** 0.25
LN_EPS = 1e-5

SUBLANES = 8
LANES = 128

FRONT = 1024
PROJ_TM = 1024
PROJ_TN = 2048
Q_COL0 = 2 * SSM_WIDTH
SSM_TM = 256
ATT_T = 256
ATT_HEADS = 2
SLAB = ATT_HEADS * SB_HEAD_DIM
ATT_QBLOCKS = 4
MIX_TM = 256
META_ROWS = 256

SSM_CHUNKS = 4
CHUNK_CH = SSM_WIDTH // SSM_CHUNKS
CHUNK_ST = SSM_GROUPS * SSM_STATE // SSM_CHUNKS
N_STATE = SSM_GROUPS * SSM_STATE
SEG_LEN = SSM_TM // SUBLANES
SEG_STEPS = (1, 2, 4)
N_POW = SEG_LEN + SUBLANES - 1

EXP2_ZERO_BELOW = -150.0
MASKED_SCORE = -1e30
LOG2_E = 1.0 / math.log(2.0)

VMEM_LIMIT = 56 * 1024 * 1024


def _resident(block_shape, index_map):
    return pl.BlockSpec(block_shape, index_map, pipeline_mode=pl.Buffered(1))


def _cmul(ar, ai, br, bi):
    return ar * br - ai * bi, ar * bi + ai * br


def _ssm_params_kernel(lr_ref, li_ref, logdt_ref, br_ref, bi_ref,
                       bbr_ref, bbi_ref, pwr_ref, pwi_ref):
    lr = lr_ref[...]
    li = li_ref[...]
    dt = jnp.exp(logdt_ref[...])
    mag = jnp.exp(lr * dt)
    ab_r = mag * jnp.cos(li * dt)
    ab_i = mag * jnp.sin(li * dt)
    pr, pi = ab_r, ab_i
    pwr_ref[0] = pr
    pwi_ref[0] = pi
    for e in range(1, SEG_LEN):
        pr, pi = _cmul(pr, pi, ab_r, ab_i)
        pwr_ref[e] = pr
        pwi_ref[e] = pi
    sr, si = pr, pi
    for k in range(1, SUBLANES):
        sr, si = _cmul(sr, si, pr, pi)
        pwr_ref[SEG_LEN - 1 + k] = sr
        pwi_ref[SEG_LEN - 1 + k] = si
    den = lr * lr + li * li
    nr = ab_r - 1.0
    f_r = (nr * lr + ab_i * li) / den
    f_i = (ab_i * lr - nr * li) / den
    br = br_ref[...]
    bi = bi_ref[...]
    bbr_ref[...] = f_r[None] * br - f_i[None] * bi
    bbi_ref[...] = f_r[None] * bi + f_i[None] * br


def _ssm_params(lam_re, lam_im, log_dt, b_re, b_im):
    g, p, c = SSM_GROUPS, SSM_STATE, SSM_GROUP
    f32 = jnp.float32
    return pl.pallas_call(
        _ssm_params_kernel,
        out_shape=(jax.ShapeDtypeStruct((c, g, p), f32),
                   jax.ShapeDtypeStruct((c, g, p), f32),
                   jax.ShapeDtypeStruct((N_POW, g, p), f32),
                   jax.ShapeDtypeStruct((N_POW, g, p), f32)),
        name="ssm_params",
    )(lam_re.astype(f32), lam_im.astype(f32), log_dt.astype(f32).reshape(g, 1),
      jnp.transpose(b_re.astype(f32), (2, 0, 1)), jnp.transpose(b_im.astype(f32), (2, 0, 1)))


def _in_proj_kernel(x_ref, meta_ref, w_ref, o_ref, xb_ref):
    j = pl.program_id(1)
    n = pl.program_id(2)

    col = n * PROJ_TN + lax.broadcasted_iota(jnp.int32, (1, PROJ_TN), 1)
    is_q = jnp.logical_and(col >= Q_COL0, col < Q_COL0 + SB_WIDTH)
    scale = jnp.where(is_q, LOG2_E / math.sqrt(SB_HEAD_DIM), 1.0).astype(jnp.float32)

    def project(rows_bf):
        acc = jnp.dot(rows_bf, w_ref[...], preferred_element_type=jnp.float32)
        return (acc * scale).astype(o_ref.dtype)

    @pl.when(j == 0)
    def _():
        pad = PROJ_TM - META_ROWS
        res = project(meta_ref[...].astype(jnp.bfloat16))
        for c in range(PROJ_TN // SLAB):
            o_ref[0, c, 0:pad, :] = jnp.zeros((pad, SLAB), o_ref.dtype)
            o_ref[0, c, pad:PROJ_TM, :] = res[:, c * SLAB:(c + 1) * SLAB]

    @pl.when(j > 0)
    def _():
        @pl.when(n == 0)
        def _():
            xb_ref[...] = x_ref[0].astype(jnp.bfloat16)

        res = project(xb_ref[...])
        for c in range(PROJ_TN // SLAB):
            o_ref[0, c] = res[:, c * SLAB:(c + 1) * SLAB]


def _in_proj(x, meta_block, w_in_bf):
    bsz, seq, _ = x.shape
    n_tiles = seq // PROJ_TM + FRONT // PROJ_TM
    lp = seq + FRONT
    return pl.pallas_call(
        _in_proj_kernel,
        grid=(bsz, n_tiles, IN_WIDTH // PROJ_TN),
        in_specs=[
            pl.BlockSpec((1, PROJ_TM, D_MODEL), lambda b, j, n: (b, jnp.maximum(j - 1, 0), 0)),
            _resident((META_ROWS, D_MODEL), lambda b, j, n: (0, 0)),
            pl.BlockSpec((D_MODEL, PROJ_TN), lambda b, j, n: (0, n)),
        ],
        out_specs=pl.BlockSpec((1, PROJ_TN // SLAB, PROJ_TM, SLAB), lambda b, j, n: (b, n, j, 0)),
        out_shape=jax.ShapeDtypeStruct((bsz, IN_WIDTH // SLAB, lp, SLAB), jnp.bfloat16),
        scratch_shapes=[pltpu.VMEM((PROJ_TM, D_MODEL), jnp.bfloat16)],
        compiler_params=pltpu.CompilerParams(
            dimension_semantics=("arbitrary", "arbitrary", "arbitrary"),
            vmem_limit_bytes=VMEM_LIMIT),
        name="in_proj",
    )(x, meta_block, w_in_bf)


def _gelu_tanh(x):
    return 0.5 * x * (1.0 + jnp.tanh(math.sqrt(2.0 / math.pi) * (x + 0.044715 * (x * x * x))))


def _sigmoid(x):
    return 1.0 / (1.0 + jnp.exp(-x))


def _ssm_kernel(un_ref, g_ref, perm_ref, permt_ref, bw_ref, cw_ref, a1r_ref, a1i_ref,
                sgr_ref, sgi_ref, sfr_ref, sfi_ref, fr_ref, fi_ref,
                d_ref, wglu_ref, bglu_ref, o_ref, sa_ref, sb_ref, ua_ref, ub_ref, carry_ref):
    s = pl.program_id(1)
    f32 = jnp.float32
    bf16 = jnp.bfloat16

    @pl.when(s == 0)
    def _():
        carry_ref[...] = jnp.zeros_like(carry_ref)
        sb_ref[...] = jnp.zeros_like(sb_ref)
        ub_ref[...] = jnp.zeros_like(ub_ref)

    def step(sw_ref, sr_ref, uw_ref, ur_ref):
        perm = perm_ref[...]

        def permuted(ref):
            return jnp.concatenate([jnp.dot(perm, ref[0, c], preferred_element_type=f32)
                                    for c in range(ref.shape[1])], axis=1)

        un_p = permuted(un_ref).astype(bf16)
        uw_ref[...] = un_p
        u_p = ur_ref[...]
        g_p = permuted(g_ref)

        row = lax.broadcasted_iota(jnp.int32, (SUBLANES, CHUNK_ST), 0)
        ys = []
        for kc in range(SSM_CHUNKS):
            sw_ref[:, kc * 2 * CHUNK_ST:(kc + 1) * 2 * CHUNK_ST] = jnp.dot(
                un_p[:, kc * CHUNK_CH:(kc + 1) * CHUNK_CH], bw_ref[kc], preferred_element_type=f32)
            re0 = kc * 2 * CHUNK_ST
            im0 = re0 + CHUNK_ST
            cm0 = kc * CHUNK_ST
            last = SSM_TM - SUBLANES
            er = sr_ref[last:SSM_TM, re0:re0 + CHUNK_ST]
            ei = sr_ref[last:SSM_TM, im0:im0 + CHUNK_ST]
            for k, d in enumerate(SEG_STEPS):
                mr = sgr_ref[k, :, cm0:cm0 + CHUNK_ST]
                mi = sgi_ref[k, :, cm0:cm0 + CHUNK_ST]
                pr = pltpu.roll(er, d, 0)
                pi = pltpu.roll(ei, d, 0)
                er, ei = er + (mr * pr - mi * pi), ei + (mr * pi + mi * pr)
            c_r = carry_ref[:, re0:re0 + CHUNK_ST]
            c_i = carry_ref[:, im0:im0 + CHUNK_ST]
            sfr = sfr_ref[:, cm0:cm0 + CHUNK_ST]
            sfi = sfi_ref[:, cm0:cm0 + CHUNK_ST]
            hr = er + (sfr * c_r - sfi * c_i)
            hi = ei + (sfr * c_i + sfi * c_r)
            in_r = jnp.where(row == 0, c_r, pltpu.roll(hr, 1, 0))
            in_i = jnp.where(row == 0, c_i, pltpu.roll(hi, 1, 0))
            carry_ref[:, re0:re0 + CHUNK_ST] = jnp.broadcast_to(hr[SUBLANES - 1:SUBLANES, :],
                                                                 (SUBLANES, CHUNK_ST))
            carry_ref[:, im0:im0 + CHUNK_ST] = jnp.broadcast_to(hi[SUBLANES - 1:SUBLANES, :],
                                                                 (SUBLANES, CHUNK_ST))
            tin_r = jnp.concatenate([in_r] * SEG_LEN, axis=0)
            tin_i = jnp.concatenate([in_i] * SEG_LEN, axis=0)
            fr = fr_ref[:, cm0:cm0 + CHUNK_ST]
            fi = fi_ref[:, cm0:cm0 + CHUNK_ST]
            h_r = sr_ref[:, re0:re0 + CHUNK_ST] + (fr * tin_r - fi * tin_i)
            h_i = sr_ref[:, im0:im0 + CHUNK_ST] + (fr * tin_i + fi * tin_r)
            h_bf = jnp.concatenate([h_r.astype(bf16), h_i.astype(bf16)], axis=1)
            ys.append(jnp.dot(h_bf, cw_ref[kc], preferred_element_type=f32))
        y = jnp.concatenate(ys, axis=1) + d_ref[...] * u_p.astype(f32)
        z = _gelu_tanh(y).astype(bf16)
        hg = jnp.dot(z, wglu_ref[...], preferred_element_type=f32) + bglu_ref[...]
        glu = hg[:, :SSM_WIDTH] * _sigmoid(hg[:, SSM_WIDTH:])
        out_p = (glu * (g_p * _sigmoid(g_p))).astype(bf16)
        o_ref[0] = jnp.dot(permt_ref[...], out_p, preferred_element_type=f32).astype(o_ref.dtype)

        lane_tiles = CHUNK_ST // LANES
        for kc in range(SSM_CHUNKS):
            cols = [(kc * 2 * CHUNK_ST + c * LANES,
                     kc * 2 * CHUNK_ST + CHUNK_ST + c * LANES,
                     kc * CHUNK_ST + c * LANES) for c in range(lane_tiles)]

            def body(t, state, cols=cols):
                r0 = pl.multiple_of(t * SUBLANES, SUBLANES)
                new = []
                for idx, (cr, ci, cm) in enumerate(cols):
                    ar = a1r_ref[:, cm:cm + LANES]
                    ai = a1i_ref[:, cm:cm + LANES]
                    xr = state[2 * idx]
                    xi = state[2 * idx + 1]
                    nr = (ar * xr - ai * xi) + sw_ref[pl.ds(r0, SUBLANES), cr:cr + LANES]
                    ni = (ar * xi + ai * xr) + sw_ref[pl.ds(r0, SUBLANES), ci:ci + LANES]
                    sw_ref[pl.ds(r0, SUBLANES), cr:cr + LANES] = nr
                    sw_ref[pl.ds(r0, SUBLANES), ci:ci + LANES] = ni
                    new.append(nr)
                    new.append(ni)
                return tuple(new)

            zero = jnp.zeros((SUBLANES, LANES), f32)
            lax.fori_loop(0, SEG_LEN, body, tuple(zero for _ in range(2 * lane_tiles)))

    @pl.when(s % 2 == 0)
    def _():
        step(sa_ref, sb_ref, ua_ref, ub_ref)

    @pl.when(s % 2 == 1)
    def _():
        step(sb_ref, sa_ref, ub_ref, ua_ref)


def _ssm(proj, perm, perm_t, bw, cw, tables, d_row, w_glu_bf, b_glu_row, seq):
    bsz = proj.shape[0]
    t0 = FRONT // SSM_TM - 1
    n_tiles = seq // SSM_TM + 1
    consts = (perm, perm_t, bw, cw) + tuple(tables) + (d_row, w_glu_bf, b_glu_row)

    def const_spec(a):
        nd = a.ndim
        return _resident(a.shape, lambda b, s, nd=nd: (0,) * nd)

    def cur(s):
        return jnp.maximum(s - 1, 0)

    ns = SSM_WIDTH // SLAB

    return pl.pallas_call(
        _ssm_kernel,
        grid=(bsz, n_tiles + 1),
        in_specs=[
            pl.BlockSpec((1, ns, SSM_TM, SLAB), lambda b, s: (b, 0, t0 + jnp.minimum(s, n_tiles - 1), 0)),
            pl.BlockSpec((1, ns, SSM_TM, SLAB), lambda b, s: (b, 1, t0 + cur(s), 0)),
        ] + [const_spec(a) for a in consts],
        out_specs=pl.BlockSpec((1, SSM_TM, SSM_WIDTH), lambda b, s: (b, jnp.maximum(s - 2, 0), 0)),
        out_shape=jax.ShapeDtypeStruct((bsz, seq, SSM_WIDTH), jnp.bfloat16),
        scratch_shapes=[pltpu.VMEM((SSM_TM, 2 * N_STATE), jnp.float32),
                        pltpu.VMEM((SSM_TM, 2 * N_STATE), jnp.float32),
                        pltpu.VMEM((SSM_TM, SSM_WIDTH), jnp.bfloat16),
                        pltpu.VMEM((SSM_TM, SSM_WIDTH), jnp.bfloat16),
                        pltpu.VMEM((SUBLANES, 2 * N_STATE), jnp.float32)],
        compiler_params=pltpu.CompilerParams(
            dimension_semantics=("arbitrary", "arbitrary"),
            vmem_limit_bytes=VMEM_LIMIT),
        name="ssm",
    )(proj, proj, *consts)


def _attention_kernel(q_ref, k_ref, v_ref, g_ref, tri_ref, o_ref, acc_ref, run_ref):
    qi = pl.program_id(2)
    t = ATT_T
    hd = SB_HEAD_DIM
    f32 = jnp.float32
    bf16 = jnp.bfloat16
    qs = [FRONT + (qi * ATT_QBLOCKS + c) * t for c in range(ATT_QBLOCKS)]
    n_blocks = (qi + 1) * ATT_QBLOCKS + 1
    row = lax.broadcasted_iota(jnp.int32, (t, t), 0)
    col = lax.broadcasted_iota(jnp.int32, (t, t), 1)

    def block(c, h, ks, run, causal):
        hs = slice(h * hd, (h + 1) * hd)
        q = q_ref[0, 0, c * t:(c + 1) * t, hs]
        kb = k_ref[0, 0, pl.ds(ks, t), hs]
        vb = v_ref[0, 0, pl.ds(ks, t), hs]
        z = lax.dot_general(q, kb, (((1,), (1,)), ((), ())), preferred_element_type=f32)
        if causal:
            z = jnp.where(col < row, z, MASKED_SCORE)
        sp = jnp.maximum(z, 0.0) + jnp.log2(1.0 + jnp.exp2(-jnp.abs(z)))
        hi = sp.astype(bf16)
        lo = (sp - hi.astype(f32)).astype(bf16)
        nsum = jnp.dot(jnp.concatenate([hi, lo], axis=1), tri_ref[...],
                       preferred_element_type=f32)
        a = jnp.exp2(z + nsum + jnp.concatenate([run] * (t // LANES), axis=1))
        pv = jnp.dot(a.astype(bf16), vb, preferred_element_type=f32)
        return pv, run + jnp.broadcast_to(nsum[:, 0:1], (t, LANES))

    def pair(c, h, ks, run, first):
        pv0, run = block(c, h, pl.multiple_of(jnp.maximum(ks, 0), t), run, first)
        pv1, run = block(c, h, pl.multiple_of(jnp.maximum(ks - t, 0), t), run, False)
        return pv0 + pv1, run

    live = jnp.bool_(False)
    for c in range(ATT_QBLOCKS):
        rows = slice(c * t, (c + 1) * t)
        for h in range(ATT_HEADS):
            lanes = slice(h * LANES, (h + 1) * LANES)
            pv, run = pair(c, h, qs[c], jnp.zeros((t, LANES), f32), True)
            acc_ref[rows, lanes] = pv
            run_ref[rows, lanes] = run
            live = jnp.logical_or(live, jnp.max(run) >= EXP2_ZERO_BELOW)

    def cond(state):
        j, live = state
        return jnp.logical_and(j < n_blocks, live)

    def body(state):
        j, _ = state
        live = jnp.bool_(False)
        for c in range(ATT_QBLOCKS):
            rows = slice(c * t, (c + 1) * t)
            for h in range(ATT_HEADS):
                lanes = slice(h * LANES, (h + 1) * LANES)
                pv, run = pair(c, h, qs[c] - j * t, run_ref[rows, lanes], False)
                acc_ref[rows, lanes] += pv
                run_ref[rows, lanes] = run
                live = jnp.logical_or(live, jnp.max(run) >= EXP2_ZERO_BELOW)
        return j + 2, live

    lax.while_loop(cond, body, (jnp.int32(2), live))
    g = g_ref[0, 0].astype(f32)
    o_ref[0] = (acc_ref[...] * (g * _sigmoid(g))).astype(o_ref.dtype)


def _attention(proj, tri, seq):
    bsz, _, lp, w = proj.shape
    t = ATT_T
    q_col0 = 2 * SSM_WIDTH // w
    k_col0 = q_col0 + SB_WIDTH // w
    v_col0 = k_col0 + SB_WIDTH // w
    g_col0 = v_col0 + SB_WIDTH // w
    tq = ATT_QBLOCKS * t
    assert FRONT % tq == 0 and FRONT >= 2 * t
    return pl.pallas_call(
        _attention_kernel,
        grid=(bsz, SB_HEADS // ATT_HEADS, seq // tq),
        in_specs=[
            pl.BlockSpec((1, 1, tq, w), lambda b, h, i: (b, q_col0 + h, FRONT // tq + i, 0)),
            pl.BlockSpec((1, 1, lp, w), lambda b, h, i: (b, k_col0 + h, 0, 0)),
            pl.BlockSpec((1, 1, lp, w), lambda b, h, i: (b, v_col0 + h, 0, 0)),
            pl.BlockSpec((1, 1, tq, w), lambda b, h, i: (b, g_col0 + h, FRONT // tq + i, 0)),
            _resident(tri.shape, lambda b, h, i: (0, 0)),
        ],
        out_specs=pl.BlockSpec((1, tq, w), lambda b, h, i: (b, i, h)),
        out_shape=jax.ShapeDtypeStruct((bsz, seq, SB_WIDTH), jnp.bfloat16),
        scratch_shapes=[pltpu.VMEM((tq, w), jnp.float32),
                        pltpu.VMEM((tq, ATT_HEADS * LANES), jnp.float32)],
        compiler_params=pltpu.CompilerParams(
            dimension_semantics=("arbitrary", "arbitrary", "arbitrary"),
            vmem_limit_bytes=VMEM_LIMIT),
        name="attention",
    )(proj, proj, proj, proj, tri)


def _mix_out_kernel(x_ref, a_ref, s_ref, wa_ref, ws_ref, wg_ref, bg_ref,
                    wo_ref, gain_ref, bias_ref, o_ref):
    f32 = jnp.float32
    x = x_ref[0]
    xb = x.astype(jnp.bfloat16)
    a = a_ref[0]
    s = s_ref[0]
    y_ssm = jnp.dot(a, wa_ref[...], preferred_element_type=f32)
    y_sb = jnp.dot(s, ws_ref[...], preferred_element_type=f32)
    gates = _sigmoid(jnp.dot(xb, wg_ref[...], preferred_element_type=f32) + bg_ref[...])
    mixed = (gates[:, :D_MODEL] * y_ssm + gates[:, D_MODEL:] * y_sb).astype(jnp.bfloat16)
    r = DEEPNORM_ALPHA * x + jnp.dot(mixed, wo_ref[...], preferred_element_type=f32)
    mu = jnp.mean(r, axis=-1, keepdims=True)
    c = r - mu
    var = jnp.mean(c * c, axis=-1, keepdims=True)
    o_ref[0] = c * lax.rsqrt(var + LN_EPS) * gain_ref[...] + bias_ref[...]


def _mix_out(x, a_ssm, a_sb, wa, ws, wg, bg_row, wo, gain_row, bias_row):
    bsz, seq, _ = x.shape
    consts = (wa, ws, wg, bg_row, wo, gain_row, bias_row)
    return pl.pallas_call(
        _mix_out_kernel,
        grid=(bsz, seq // MIX_TM),
        in_specs=[
            pl.BlockSpec((1, MIX_TM, D_MODEL), lambda b, j: (b, j, 0)),
            pl.BlockSpec((1, MIX_TM, SSM_WIDTH), lambda b, j: (b, j, 0)),
            pl.BlockSpec((1, MIX_TM, SB_WIDTH), lambda b, j: (b, j, 0)),
        ] + [_resident(w.shape, lambda b, j: (0, 0)) for w in consts],
        out_specs=pl.BlockSpec((1, MIX_TM, D_MODEL), lambda b, j: (b, j, 0)),
        out_shape=jax.ShapeDtypeStruct((bsz, seq, D_MODEL), jnp.float32),
        compiler_params=pltpu.CompilerParams(
            dimension_semantics=("arbitrary", "arbitrary"),
            vmem_limit_bytes=VMEM_LIMIT),
        name="mix_out",
    )(x, a_ssm, a_sb, *consts)


def _block_diag_b(bb_r, bb_i):
    gl = SSM_GROUPS // SSM_CHUNKS
    eye = jnp.eye(gl, dtype=jnp.float32)

    def part(bb):
        b4 = jnp.transpose(bb, (1, 0, 2)).reshape(SSM_CHUNKS, gl, SSM_GROUP, SSM_STATE)
        return jnp.einsum('kgcp,gh->kgchp', b4, eye).reshape(SSM_CHUNKS, CHUNK_CH, CHUNK_ST)

    return jnp.concatenate([part(bb_r), part(bb_i)], axis=2).astype(jnp.bfloat16)


def _block_diag_c(c_re, c_im):
    gl = SSM_GROUPS // SSM_CHUNKS
    eye = jnp.eye(gl, dtype=jnp.float32)

    def part(c):
        c4 = c.astype(jnp.float32).reshape(SSM_CHUNKS, gl, SSM_GROUP, SSM_STATE)
        return jnp.einsum('kgcp,gh->kgphc', c4, eye).reshape(SSM_CHUNKS, CHUNK_ST, CHUNK_CH)

    return jnp.concatenate([part(c_re), -part(c_im)], axis=1).astype(jnp.bfloat16)


def _scan_tables(pw_r, pw_i):
    rows = jnp.arange(SUBLANES)[:, None]

    def tables(pw):
        p = pw.reshape(N_POW, N_STATE)
        a1 = jnp.broadcast_to(p[0][None, :], (SUBLANES, N_STATE))
        seg = p[SEG_LEN - 1:]
        sg = jnp.stack([jnp.where(rows >= d, seg[d - 1][None, :], 0.0) for d in SEG_STEPS])
        f = jnp.repeat(p[:SEG_LEN], SUBLANES, axis=0)
        return a1, sg, seg, f

    a1r, sgr, sfr, fr = tables(pw_r)
    a1i, sgi, sfi, fi = tables(pw_i)
    return a1r, a1i, sgr, sgi, sfr, sfi, fr, fi


def _scan_permutation():
    r = np.arange(SSM_TM)
    tok = (r % SUBLANES) * SEG_LEN + r // SUBLANES
    perm = (tok[:, None] == r[None, :]).astype(np.float32)
    return jnp.asarray(perm, jnp.bfloat16), jnp.asarray(perm.T, jnp.bfloat16)


def kernel(x, meta_tokens, w_in, ssm_lambda_re, ssm_lambda_im, ssm_log_dt, ssm_b_re, ssm_b_im,
           ssm_c_re, ssm_c_im, ssm_d, w_glu, b_glu, w_branch_ssm, w_branch_sb, w_gate, b_gate,
           w_out, ln_gain, ln_bias):
    bsz, seq, _ = x.shape
    f32 = jnp.float32
    bf16 = jnp.bfloat16
    layer = 0

    bb_r, bb_i, pw_r, pw_i = _ssm_params(ssm_lambda_re[layer], ssm_lambda_im[layer],
                                         ssm_log_dt[layer], ssm_b_re[layer], ssm_b_im[layer])
    bw = _block_diag_b(bb_r, bb_i)
    cw = _block_diag_c(ssm_c_re[layer], ssm_c_im[layer])
    tables = _scan_tables(pw_r, pw_i)
    perm, perm_t = _scan_permutation()

    meta_block = jnp.concatenate(
        [jnp.zeros((META_ROWS - N_META, D_MODEL), f32), meta_tokens.astype(f32)], axis=0)
    proj = _in_proj(x.astype(f32), meta_block, w_in[layer].astype(bf16))

    a_ssm = _ssm(proj, perm, perm_t, bw, cw, tables,
                 ssm_d[layer].astype(f32).reshape(1, SSM_WIDTH),
                 w_glu[layer].astype(bf16), b_glu[layer].astype(f32).reshape(1, 2 * SSM_WIDTH), seq)

    t = ATT_T
    r = np.arange(t)
    neg_lower = -(r[:, None] >= r[None, :]).astype(np.float32)
    tri = jnp.asarray(np.concatenate([neg_lower, neg_lower], axis=0), bf16)
    a_sb = _attention(proj, tri, seq)

    return _mix_out(x.astype(f32), a_ssm, a_sb,
                    w_branch_ssm[layer].astype(bf16), w_branch_sb[layer].astype(bf16),
                    w_gate[layer].astype(bf16), b_gate[layer].astype(f32).reshape(1, 2 * D_MODEL),
                    w_out[layer].astype(bf16),
                    ln_gain[layer].astype(f32).reshape(1, D_MODEL),
                    ln_bias[layer].astype(f32).reshape(1, D_MODEL))
```

```python
import math

import jax
import jax.numpy as jnp
import numpy as np
from jax import lax
from jax.experimental import pallas as pl
from jax.experimental.pallas import tpu as pltpu

D_MODEL = 2048
N_META = 16
SSM_WIDTH = 1024
SSM_GROUP = 16
SSM_GROUPS = 64
SSM_STATE = 64
SB_WIDTH = 1024
SB_HEADS = 8
SB_HEAD_DIM = 128
IN_WIDTH = 2 * SSM_WIDTH + 4 * SB_WIDTH
DEEPNORM_ALPHA = 2.0 ** 0.25
LN_EPS = 1e-5

SUBLANES = 8
LANES = 128

FRONT = 2048
PROJ_TM = 1024
PROJ_TN = 2048
Q_COL0 = 2 * SSM_WIDTH
SSM_TM = 256
ATT_T = 256
ATT_HEADS = 2
SLAB = ATT_HEADS * SB_HEAD_DIM
ATT_QBLOCKS = 8
MIX_TM = 256
META_ROWS = 256

SSM_CHUNKS = 4
CHUNK_CH = SSM_WIDTH // SSM_CHUNKS
CHUNK_ST = SSM_GROUPS * SSM_STATE // SSM_CHUNKS
N_STATE = SSM_GROUPS * SSM_STATE
SEG_LEN = SSM_TM // SUBLANES
SEG_STEPS = (1, 2, 4)
N_POW = SEG_LEN + SUBLANES - 1

EXP2_ZERO_BELOW = -150.0
MASKED_SCORE = -1e30
LOG2_E = 1.0 / math.log(2.0)

VMEM_LIMIT = 56 * 1024 * 1024


def _resident(block_shape, index_map):
    return pl.BlockSpec(block_shape, index_map, pipeline_mode=pl.Buffered(1))


def _cmul(ar, ai, br, bi):
    return ar * br - ai * bi, ar * bi + ai * br


def _ssm_params_kernel(lr_ref, li_ref, logdt_ref, br_ref, bi_ref,
                       bbr_ref, bbi_ref, pwr_ref, pwi_ref):
    lr = lr_ref[...]
    li = li_ref[...]
    dt = jnp.exp(logdt_ref[...])
    mag = jnp.exp(lr * dt)
    ab_r = mag * jnp.cos(li * dt)
    ab_i = mag * jnp.sin(li * dt)
    pr, pi = ab_r, ab_i
    pwr_ref[0] = pr
    pwi_ref[0] = pi
    for e in range(1, SEG_LEN):
        pr, pi = _cmul(pr, pi, ab_r, ab_i)
        pwr_ref[e] = pr
        pwi_ref[e] = pi
    sr, si = pr, pi
    for k in range(1, SUBLANES):
        sr, si = _cmul(sr, si, pr, pi)
        pwr_ref[SEG_LEN - 1 + k] = sr
        pwi_ref[SEG_LEN - 1 + k] = si
    den = lr * lr + li * li
    nr = ab_r - 1.0
    f_r = (nr * lr + ab_i * li) / den
    f_i = (ab_i * lr - nr * li) / den
    br = br_ref[...]
    bi = bi_ref[...]
    bbr_ref[...] = f_r[None] * br - f_i[None] * bi
    bbi_ref[...] = f_r[None] * bi + f_i[None] * br


def _ssm_params(lam_re, lam_im, log_dt, b_re, b_im):
    g, p, c = SSM_GROUPS, SSM_STATE, SSM_GROUP
    f32 = jnp.float32
    return pl.pallas_call(
        _ssm_params_kernel,
        out_shape=(jax.ShapeDtypeStruct((c, g, p), f32),
                   jax.ShapeDtypeStruct((c, g, p), f32),
                   jax.ShapeDtypeStruct((N_POW, g, p), f32),
                   jax.ShapeDtypeStruct((N_POW, g, p), f32)),
        name="ssm_params",
    )(lam_re.astype(f32), lam_im.astype(f32), log_dt.astype(f32).reshape(g, 1),
      jnp.transpose(b_re.astype(f32), (2, 0, 1)), jnp.transpose(b_im.astype(f32), (2, 0, 1)))


def _in_proj_kernel(x_ref, meta_ref, w_ref, o_ref, xb_ref):
    j = pl.program_id(1)
    n = pl.program_id(2)

    col = n * PROJ_TN + lax.broadcasted_iota(jnp.int32, (1, PROJ_TN), 1)
    is_q = jnp.logical_and(col >= Q_COL0, col < Q_COL0 + SB_WIDTH)
    scale = jnp.where(is_q, LOG2_E / math.sqrt(SB_HEAD_DIM), 1.0).astype(jnp.float32)

    def project(rows_bf):
        acc = jnp.dot(rows_bf, w_ref[...], preferred_element_type=jnp.float32)
        return (acc * scale).astype(o_ref.dtype)

    n_front = FRONT // PROJ_TM

    @pl.when(j < n_front - 1)
    def _():
        o_ref[...] = jnp.zeros_like(o_ref)

    @pl.when(j == n_front - 1)
    def _():
        pad = PROJ_TM - META_ROWS
        res = project(meta_ref[...].astype(jnp.bfloat16))
        for c in range(PROJ_TN // SLAB):
            o_ref[0, c, 0:pad, :] = jnp.zeros((pad, SLAB), o_ref.dtype)
            o_ref[0, c, pad:PROJ_TM, :] = res[:, c * SLAB:(c + 1) * SLAB]

    @pl.when(j >= n_front)
    def _():
        @pl.when(n == 0)
        def _():
            xb_ref[...] = x_ref[0].astype(jnp.bfloat16)

        res = project(xb_ref[...])
        for c in range(PROJ_TN // SLAB):
            o_ref[0, c] = res[:, c * SLAB:(c + 1) * SLAB]


def _in_proj(x, meta_block, w_in_bf):
    bsz, seq, _ = x.shape
    n_tiles = seq // PROJ_TM + FRONT // PROJ_TM
    lp = seq + FRONT
    return pl.pallas_call(
        _in_proj_kernel,
        grid=(bsz, n_tiles, IN_WIDTH // PROJ_TN),
        in_specs=[
            pl.BlockSpec((1, PROJ_TM, D_MODEL),
                         lambda b, j, n: (b, jnp.maximum(j - FRONT // PROJ_TM, 0), 0)),
            _resident((META_ROWS, D_MODEL), lambda b, j, n: (0, 0)),
            pl.BlockSpec((D_MODEL, PROJ_TN), lambda b, j, n: (0, n)),
        ],
        out_specs=pl.BlockSpec((1, PROJ_TN // SLAB, PROJ_TM, SLAB), lambda b, j, n: (b, n, j, 0)),
        out_shape=jax.ShapeDtypeStruct((bsz, IN_WIDTH // SLAB, lp, SLAB), jnp.bfloat16),
        scratch_shapes=[pltpu.VMEM((PROJ_TM, D_MODEL), jnp.bfloat16)],
        compiler_params=pltpu.CompilerParams(
            dimension_semantics=("arbitrary", "arbitrary", "arbitrary"),
            vmem_limit_bytes=VMEM_LIMIT),
        name="in_proj",
    )(x, meta_block, w_in_bf)


def _gelu_tanh(x):
    return 0.5 * x * (1.0 + jnp.tanh(math.sqrt(2.0 / math.pi) * (x + 0.044715 * (x * x * x))))


def _sigmoid(x):
    return 1.0 / (1.0 + jnp.exp(-x))


def _ssm_kernel(un_ref, g_ref, perm_ref, permt_ref, bw_ref, cw_ref, a1r_ref, a1i_ref,
                sgr_ref, sgi_ref, sfr_ref, sfi_ref, fr_ref, fi_ref,
                d_ref, wglu_ref, bglu_ref, o_ref, sa_ref, sb_ref, ua_ref, ub_ref, carry_ref):
    s = pl.program_id(1)
    f32 = jnp.float32
    bf16 = jnp.bfloat16

    @pl.when(s == 0)
    def _():
        carry_ref[...] = jnp.zeros_like(carry_ref)
        sb_ref[...] = jnp.zeros_like(sb_ref)
        ub_ref[...] = jnp.zeros_like(ub_ref)

    def step(sw_ref, sr_ref, uw_ref, ur_ref):
        perm = perm_ref[...]

        def permuted(ref):
            return jnp.concatenate([jnp.dot(perm, ref[0, c], preferred_element_type=f32)
                                    for c in range(ref.shape[1])], axis=1)

        un_p = permuted(un_ref).astype(bf16)
        uw_ref[...] = un_p
        u_p = ur_ref[...]
        g_p = permuted(g_ref)

        row = lax.broadcasted_iota(jnp.int32, (SUBLANES, CHUNK_ST), 0)
        ys = []
        for kc in range(SSM_CHUNKS):
            sw_ref[:, kc * 2 * CHUNK_ST:(kc + 1) * 2 * CHUNK_ST] = jnp.dot(
                un_p[:, kc * CHUNK_CH:(kc + 1) * CHUNK_CH], bw_ref[kc], preferred_element_type=f32)
            re0 = kc * 2 * CHUNK_ST
            im0 = re0 + CHUNK_ST
            cm0 = kc * CHUNK_ST
            last = SSM_TM - SUBLANES
            er = sr_ref[last:SSM_TM, re0:re0 + CHUNK_ST]
            ei = sr_ref[last:SSM_TM, im0:im0 + CHUNK_ST]
            for k, d in enumerate(SEG_STEPS):
                mr = sgr_ref[k, :, cm0:cm0 + CHUNK_ST]
                mi = sgi_ref[k, :, cm0:cm0 + CHUNK_ST]
                pr = pltpu.roll(er, d, 0)
                pi = pltpu.roll(ei, d, 0)
                er, ei = er + (mr * pr - mi * pi), ei + (mr * pi + mi * pr)
            c_r = carry_ref[:, re0:re0 + CHUNK_ST]
            c_i = carry_ref[:, im0:im0 + CHUNK_ST]
            sfr = sfr_ref[:, cm0:cm0 + CHUNK_ST]
            sfi = sfi_ref[:, cm0:cm0 + CHUNK_ST]
            hr = er + (sfr * c_r - sfi * c_i)
            hi = ei + (sfr * c_i + sfi * c_r)
            in_r = jnp.where(row == 0, c_r, pltpu.roll(hr, 1, 0))
            in_i = jnp.where(row == 0, c_i, pltpu.roll(hi, 1, 0))
            carry_ref[:, re0:re0 + CHUNK_ST] = jnp.broadcast_to(hr[SUBLANES - 1:SUBLANES, :],
                                                                 (SUBLANES, CHUNK_ST))
            carry_ref[:, im0:im0 + CHUNK_ST] = jnp.broadcast_to(hi[SUBLANES - 1:SUBLANES, :],
                                                                 (SUBLANES, CHUNK_ST))
            tin_r = jnp.concatenate([in_r] * SEG_LEN, axis=0)
            tin_i = jnp.concatenate([in_i] * SEG_LEN, axis=0)
            fr = fr_ref[:, cm0:cm0 + CHUNK_ST]
            fi = fi_ref[:, cm0:cm0 + CHUNK_ST]
            h_r = sr_ref[:, re0:re0 + CHUNK_ST] + (fr * tin_r - fi * tin_i)
            h_i = sr_ref[:, im0:im0 + CHUNK_ST] + (fr * tin_i + fi * tin_r)
            h_bf = jnp.concatenate([h_r.astype(bf16), h_i.astype(bf16)], axis=1)
            ys.append(jnp.dot(h_bf, cw_ref[kc], preferred_element_type=f32))
        y = jnp.concatenate(ys, axis=1) + d_ref[...] * u_p.astype(f32)
        z = _gelu_tanh(y).astype(bf16)
        hg = jnp.dot(z, wglu_ref[...], preferred_element_type=f32) + bglu_ref[...]
        glu = hg[:, :SSM_WIDTH] * _sigmoid(hg[:, SSM_WIDTH:])
        out_p = (glu * (g_p * _sigmoid(g_p))).astype(bf16)
        o_ref[0] = jnp.dot(permt_ref[...], out_p, preferred_element_type=f32).astype(o_ref.dtype)

        lane_tiles = CHUNK_ST // LANES
        for kc in range(SSM_CHUNKS):
            cols = [(kc * 2 * CHUNK_ST + c * LANES,
                     kc * 2 * CHUNK_ST + CHUNK_ST + c * LANES,
                     kc * CHUNK_ST + c * LANES) for c in range(lane_tiles)]

            def body(t, state, cols=cols):
                r0 = pl.multiple_of(t * SUBLANES, SUBLANES)
                new = []
                for idx, (cr, ci, cm) in enumerate(cols):
                    ar = a1r_ref[:, cm:cm + LANES]
                    ai = a1i_ref[:, cm:cm + LANES]
                    xr = state[2 * idx]
                    xi = state[2 * idx + 1]
                    nr = (ar * xr - ai * xi) + sw_ref[pl.ds(r0, SUBLANES), cr:cr + LANES]
                    ni = (ar * xi + ai * xr) + sw_ref[pl.ds(r0, SUBLANES), ci:ci + LANES]
                    sw_ref[pl.ds(r0, SUBLANES), cr:cr + LANES] = nr
                    sw_ref[pl.ds(r0, SUBLANES), ci:ci + LANES] = ni
                    new.append(nr)
                    new.append(ni)
                return tuple(new)

            zero = jnp.zeros((SUBLANES, LANES), f32)
            lax.fori_loop(0, SEG_LEN, body, tuple(zero for _ in range(2 * lane_tiles)), unroll=8)

    @pl.when(s % 2 == 0)
    def _():
        step(sa_ref, sb_ref, ua_ref, ub_ref)

    @pl.when(s % 2 == 1)
    def _():
        step(sb_ref, sa_ref, ub_ref, ua_ref)


def _ssm(proj, perm, perm_t, bw, cw, tables, d_row, w_glu_bf, b_glu_row, seq):
    bsz = proj.shape[0]
    t0 = FRONT // SSM_TM - 1
    n_tiles = seq // SSM_TM + 1
    consts = (perm, perm_t, bw, cw) + tuple(tables) + (d_row, w_glu_bf, b_glu_row)

    def const_spec(a):
        nd = a.ndim
        return _resident(a.shape, lambda b, s, nd=nd: (0,) * nd)

    def cur(s):
        return jnp.maximum(s - 1, 0)

    ns = SSM_WIDTH // SLAB

    return pl.pallas_call(
        _ssm_kernel,
        grid=(bsz, n_tiles + 1),
        in_specs=[
            pl.BlockSpec((1, ns, SSM_TM, SLAB), lambda b, s: (b, 0, t0 + jnp.minimum(s, n_tiles - 1), 0)),
            pl.BlockSpec((1, ns, SSM_TM, SLAB), lambda b, s: (b, 1, t0 + cur(s), 0)),
        ] + [const_spec(a) for a in consts],
        out_specs=pl.BlockSpec((1, SSM_TM, SSM_WIDTH), lambda b, s: (b, jnp.maximum(s - 2, 0), 0)),
        out_shape=jax.ShapeDtypeStruct((bsz, seq, SSM_WIDTH), jnp.bfloat16),
        scratch_shapes=[pltpu.VMEM((SSM_TM, 2 * N_STATE), jnp.float32),
                        pltpu.VMEM((SSM_TM, 2 * N_STATE), jnp.float32),
                        pltpu.VMEM((SSM_TM, SSM_WIDTH), jnp.bfloat16),
                        pltpu.VMEM((SSM_TM, SSM_WIDTH), jnp.bfloat16),
                        pltpu.VMEM((SUBLANES, 2 * N_STATE), jnp.float32)],
        compiler_params=pltpu.CompilerParams(
            dimension_semantics=("arbitrary", "arbitrary"),
            vmem_limit_bytes=VMEM_LIMIT),
        name="ssm",
    )(proj, proj, *consts)


def _attention_kernel(q_ref, k_ref, v_ref, g_ref, tri_ref, o_ref, acc_ref, run_ref):
    qi = pl.program_id(2)
    t = ATT_T
    hd = SB_HEAD_DIM
    f32 = jnp.float32
    bf16 = jnp.bfloat16
    qs = [FRONT + (qi * ATT_QBLOCKS + c) * t for c in range(ATT_QBLOCKS)]
    n_blocks = (qi + 1) * ATT_QBLOCKS + 1
    row = lax.broadcasted_iota(jnp.int32, (t, t), 0)
    col = lax.broadcasted_iota(jnp.int32, (t, t), 1)

    def block(c, h, ks, run, causal):
        hs = slice(h * hd, (h + 1) * hd)
        q = q_ref[0, 0, c * t:(c + 1) * t, hs]
        kb = k_ref[0, 0, pl.ds(ks, t), hs]
        vb = v_ref[0, 0, pl.ds(ks, t), hs]
        z = lax.dot_general(q, kb, (((1,), (1,)), ((), ())), preferred_element_type=f32)
        if causal:
            z = jnp.where(col < row, z, MASKED_SCORE)
        sp = jnp.maximum(z, 0.0) + jnp.log2(1.0 + jnp.exp2(-jnp.abs(z)))
        hi = sp.astype(bf16)
        lo = (sp - hi.astype(f32)).astype(bf16)
        nsum = jnp.dot(jnp.concatenate([hi, lo], axis=1), tri_ref[...],
                       preferred_element_type=f32)
        a = jnp.exp2(z + nsum + jnp.concatenate([run] * (t // LANES), axis=1))
        pv = jnp.dot(a.astype(bf16), vb, preferred_element_type=f32)
        return pv, run + jnp.broadcast_to(nsum[:, 0:1], (t, LANES))

    def pair(c, h, ks, run, first):
        pv0, run = block(c, h, pl.multiple_of(jnp.maximum(ks, 0), t), run, first)
        pv1, run = block(c, h, pl.multiple_of(jnp.maximum(ks - t, 0), t), run, False)
        return pv0 + pv1, run

    live = jnp.bool_(False)
    for c in range(ATT_QBLOCKS):
        rows = slice(c * t, (c + 1) * t)
        for h in range(ATT_HEADS):
            lanes = slice(h * LANES, (h + 1) * LANES)
            pv, run = pair(c, h, qs[c], jnp.zeros((t, LANES), f32), True)
            acc_ref[rows, lanes] = pv
            run_ref[rows, lanes] = run
            live = jnp.logical_or(live, jnp.max(run) >= EXP2_ZERO_BELOW)

    def cond(state):
        j, live = state
        return jnp.logical_and(j < n_blocks, live)

    def body(state):
        j, _ = state
        live = jnp.bool_(False)
        for c in range(ATT_QBLOCKS):
            rows = slice(c * t, (c + 1) * t)
            for h in range(ATT_HEADS):
                lanes = slice(h * LANES, (h + 1) * LANES)
                pv, run = pair(c, h, qs[c] - j * t, run_ref[rows, lanes], False)
                acc_ref[rows, lanes] += pv
                run_ref[rows, lanes] = run
                live = jnp.logical_or(live, jnp.max(run) >= EXP2_ZERO_BELOW)
        return j + 2, live

    lax.while_loop(cond, body, (jnp.int32(2), live))
    g = g_ref[0, 0].astype(f32)
    o_ref[0] = (acc_ref[...] * (g * _sigmoid(g))).astype(o_ref.dtype)


def _attention(proj, tri, seq):
    bsz, _, lp, w = proj.shape
    t = ATT_T
    q_col0 = 2 * SSM_WIDTH // w
    k_col0 = q_col0 + SB_WIDTH // w
    v_col0 = k_col0 + SB_WIDTH // w
    g_col0 = v_col0 + SB_WIDTH // w
    tq = ATT_QBLOCKS * t
    assert FRONT % tq == 0 and FRONT >= 2 * t
    return pl.pallas_call(
        _attention_kernel,
        grid=(bsz, SB_HEADS // ATT_HEADS, seq // tq),
        in_specs=[
            pl.BlockSpec((1, 1, tq, w), lambda b, h, i: (b, q_col0 + h, FRONT // tq + i, 0)),
            pl.BlockSpec((1, 1, lp, w), lambda b, h, i: (b, k_col0 + h, 0, 0)),
            pl.BlockSpec((1, 1, lp, w), lambda b, h, i: (b, v_col0 + h, 0, 0)),
            pl.BlockSpec((1, 1, tq, w), lambda b, h, i: (b, g_col0 + h, FRONT // tq + i, 0)),
            _resident(tri.shape, lambda b, h, i: (0, 0)),
        ],
        out_specs=pl.BlockSpec((1, tq, w), lambda b, h, i: (b, i, h)),
        out_shape=jax.ShapeDtypeStruct((bsz, seq, SB_WIDTH), jnp.bfloat16),
        scratch_shapes=[pltpu.VMEM((tq, w), jnp.float32),
                        pltpu.VMEM((tq, ATT_HEADS * LANES), jnp.float32)],
        compiler_params=pltpu.CompilerParams(
            dimension_semantics=("arbitrary", "arbitrary", "arbitrary"),
            vmem_limit_bytes=VMEM_LIMIT),
        name="attention",
    )(proj, proj, proj, proj, tri)


def _mix_out_kernel(x_ref, a_ref, s_ref, wa_ref, ws_ref, wg_ref, bg_ref,
                    wo_ref, gain_ref, bias_ref, o_ref):
    f32 = jnp.float32
    x = x_ref[0]
    xb = x.astype(jnp.bfloat16)
    a = a_ref[0]
    s = s_ref[0]
    y_ssm = jnp.dot(a, wa_ref[...], preferred_element_type=f32)
    y_sb = jnp.dot(s, ws_ref[...], preferred_element_type=f32)
    gates = _sigmoid(jnp.dot(xb, wg_ref[...], preferred_element_type=f32) + bg_ref[...])
    mixed = (gates[:, :D_MODEL] * y_ssm + gates[:, D_MODEL:] * y_sb).astype(jnp.bfloat16)
    r = DEEPNORM_ALPHA * x + jnp.dot(mixed, wo_ref[...], preferred_element_type=f32)
    mu = jnp.mean(r, axis=-1, keepdims=True)
    c = r - mu
    var = jnp.mean(c * c, axis=-1, keepdims=True)
    o_ref[0] = c * lax.rsqrt(var + LN_EPS) * gain_ref[...] + bias_ref[...]


def _mix_out(x, a_ssm, a_sb, wa, ws, wg, bg_row, wo, gain_row, bias_row):
    bsz, seq, _ = x.shape
    consts = (wa, ws, wg, bg_row, wo, gain_row, bias_row)
    return pl.pallas_call(
        _mix_out_kernel,
        grid=(bsz, seq // MIX_TM),
        in_specs=[
            pl.BlockSpec((1, MIX_TM, D_MODEL), lambda b, j: (b, j, 0)),
            pl.BlockSpec((1, MIX_TM, SSM_WIDTH), lambda b, j: (b, j, 0)),
            pl.BlockSpec((1, MIX_TM, SB_WIDTH), lambda b, j: (b, j, 0)),
        ] + [_resident(w.shape, lambda b, j: (0, 0)) for w in consts],
        out_specs=pl.BlockSpec((1, MIX_TM, D_MODEL), lambda b, j: (b, j, 0)),
        out_shape=jax.ShapeDtypeStruct((bsz, seq, D_MODEL), jnp.float32),
        compiler_params=pltpu.CompilerParams(
            dimension_semantics=("arbitrary", "arbitrary"),
            vmem_limit_bytes=VMEM_LIMIT),
        name="mix_out",
    )(x, a_ssm, a_sb, *consts)


def _block_diag_b(bb_r, bb_i):
    gl = SSM_GROUPS // SSM_CHUNKS
    eye = jnp.eye(gl, dtype=jnp.float32)

    def part(bb):
        b4 = jnp.transpose(bb, (1, 0, 2)).reshape(SSM_CHUNKS, gl, SSM_GROUP, SSM_STATE)
        return jnp.einsum('kgcp,gh->kgchp', b4, eye).reshape(SSM_CHUNKS, CHUNK_CH, CHUNK_ST)

    return jnp.concatenate([part(bb_r), part(bb_i)], axis=2).astype(jnp.bfloat16)


def _block_diag_c(c_re, c_im):
    gl = SSM_GROUPS // SSM_CHUNKS
    eye = jnp.eye(gl, dtype=jnp.float32)

    def part(c):
        c4 = c.astype(jnp.float32).reshape(SSM_CHUNKS, gl, SSM_GROUP, SSM_STATE)
        return jnp.einsum('kgcp,gh->kgphc', c4, eye).reshape(SSM_CHUNKS, CHUNK_ST, CHUNK_CH)

    return jnp.concatenate([part(c_re), -part(c_im)], axis=1).astype(jnp.bfloat16)


def _scan_tables(pw_r, pw_i):
    rows = jnp.arange(SUBLANES)[:, None]

    def tables(pw):
        p = pw.reshape(N_POW, N_STATE)
        a1 = jnp.broadcast_to(p[0][None, :], (SUBLANES, N_STATE))
        seg = p[SEG_LEN - 1:]
        sg = jnp.stack([jnp.where(rows >= d, seg[d - 1][None, :], 0.0) for d in SEG_STEPS])
        f = jnp.repeat(p[:SEG_LEN], SUBLANES, axis=0)
        return a1, sg, seg, f

    a1r, sgr, sfr, fr = tables(pw_r)
    a1i, sgi, sfi, fi = tables(pw_i)
    return a1r, a1i, sgr, sgi, sfr, sfi, fr, fi


def _scan_permutation():
    r = np.arange(SSM_TM)
    tok = (r % SUBLANES) * SEG_LEN + r // SUBLANES
    perm = (tok[:, None] == r[None, :]).astype(np.float32)
    return jnp.asarray(perm, jnp.bfloat16), jnp.asarray(perm.T, jnp.bfloat16)


def kernel(x, meta_tokens, w_in, ssm_lambda_re, ssm_lambda_im, ssm_log_dt, ssm_b_re, ssm_b_im,
           ssm_c_re, ssm_c_im, ssm_d, w_glu, b_glu, w_branch_ssm, w_branch_sb, w_gate, b_gate,
           w_out, ln_gain, ln_bias):
    bsz, seq, _ = x.shape
    f32 = jnp.float32
    bf16 = jnp.bfloat16
    layer = 0

    bb_r, bb_i, pw_r, pw_i = _ssm_params(ssm_lambda_re[layer], ssm_lambda_im[layer],
                                         ssm_log_dt[layer], ssm_b_re[layer], ssm_b_im[layer])
    bw = _block_diag_b(bb_r, bb_i)
    cw = _block_diag_c(ssm_c_re[layer], ssm_c_im[layer])
    tables = _scan_tables(pw_r, pw_i)
    perm, perm_t = _scan_permutation()

    meta_block = jnp.concatenate(
        [jnp.zeros((META_ROWS - N_META, D_MODEL), f32), meta_tokens.astype(f32)], axis=0)
    proj = _in_proj(x.astype(f32), meta_block, w_in[layer].astype(bf16))

    a_ssm = _ssm(proj, perm, perm_t, bw, cw, tables,
                 ssm_d[layer].astype(f32).reshape(1, SSM_WIDTH),
                 w_glu[layer].astype(bf16), b_glu[layer].astype(f32).reshape(1, 2 * SSM_WIDTH), seq)

    t = ATT_T
    r = np.arange(t)
    neg_lower = -(r[:, None] >= r[None, :]).astype(np.float32)
    tri = jnp.asarray(np.concatenate([neg_lower, neg_lower], axis=0), bf16)
    a_sb = _attention(proj, tri, seq)

    return _mix_out(x.astype(f32), a_ssm, a_sb,
                    w_branch_ssm[layer].astype(bf16), w_branch_sb[layer].astype(bf16),
                    w_gate[layer].astype(bf16), b_gate[layer].astype(f32).reshape(1, 2 * D_MODEL),
                    w_out[layer].astype(bf16),
                    ln_gain[layer].astype(f32).reshape(1, D_MODEL),
                    ln_bias[layer].astype(f32).reshape(1, D_MODEL))
```

```python
import math

import jax
import jax.numpy as jnp
import numpy as np
from jax import lax
from jax.experimental import pallas as pl
from jax.experimental.pallas import tpu as pltpu

D_MODEL = 2048
N_META = 16
SSM_WIDTH = 1024
SSM_GROUP = 16
SSM_GROUPS = 64
SSM_STATE = 64
SB_WIDTH = 1024
SB_HEADS = 8
SB_HEAD_DIM = 128
IN_WIDTH = 2 * SSM_WIDTH + 4 * SB_WIDTH
DEEPNORM_ALPHA = 2.0 ** 0.25
LN_EPS = 1e-5

SUBLANES = 8
LANES = 128

FRONT = 1024
PROJ_TM = 1024
FIRST_ROW = FRONT - PROJ_TM
PROJ_TN = 2048
Q_COL0 = 2 * SSM_WIDTH
SSM_TM = 256
ATT_T = 256
ATT_HEADS = 2
SLAB = ATT_HEADS * SB_HEAD_DIM
ATT_QBLOCKS = 8
ATT_QPART = min(ATT_QBLOCKS * ATT_T, FRONT)
ATT_QPARTS = ATT_QBLOCKS * ATT_T // ATT_QPART
MIX_TM = 256
META_ROWS = 256

SSM_CHUNKS = 4
CHUNK_CH = SSM_WIDTH // SSM_CHUNKS
CHUNK_ST = SSM_GROUPS * SSM_STATE // SSM_CHUNKS
N_STATE = SSM_GROUPS * SSM_STATE
SEG_LEN = SSM_TM // SUBLANES
SEG_STEPS = (1, 2, 4)
N_POW = SEG_LEN + SUBLANES - 1

EXP2_ZERO_BELOW = -150.0
MASKED_SCORE = -1e30
LOG2_E = 1.0 / math.log(2.0)

VMEM_LIMIT = 56 * 1024 * 1024


def _resident(block_shape, index_map):
    return pl.BlockSpec(block_shape, index_map, pipeline_mode=pl.Buffered(1))


def _cmul(ar, ai, br, bi):
    return ar * br - ai * bi, ar * bi + ai * br


def _ssm_params_kernel(lr_ref, li_ref, logdt_ref, br_ref, bi_ref,
                       bbr_ref, bbi_ref, pwr_ref, pwi_ref):
    lr = lr_ref[...]
    li = li_ref[...]
    dt = jnp.exp(logdt_ref[...])
    mag = jnp.exp(lr * dt)
    ab_r = mag * jnp.cos(li * dt)
    ab_i = mag * jnp.sin(li * dt)
    pr, pi = ab_r, ab_i
    pwr_ref[0] = pr
    pwi_ref[0] = pi
    for e in range(1, SEG_LEN):
        pr, pi = _cmul(pr, pi, ab_r, ab_i)
        pwr_ref[e] = pr
        pwi_ref[e] = pi
    sr, si = pr, pi
    for k in range(1, SUBLANES):
        sr, si = _cmul(sr, si, pr, pi)
        pwr_ref[SEG_LEN - 1 + k] = sr
        pwi_ref[SEG_LEN - 1 + k] = si
    den = lr * lr + li * li
    nr = ab_r - 1.0
    f_r = (nr * lr + ab_i * li) / den
    f_i = (ab_i * lr - nr * li) / den
    br = br_ref[...]
    bi = bi_ref[...]
    bbr_ref[...] = f_r[None] * br - f_i[None] * bi
    bbi_ref[...] = f_r[None] * bi + f_i[None] * br


def _ssm_params(lam_re, lam_im, log_dt, b_re, b_im):
    g, p, c = SSM_GROUPS, SSM_STATE, SSM_GROUP
    f32 = jnp.float32
    return pl.pallas_call(
        _ssm_params_kernel,
        out_shape=(jax.ShapeDtypeStruct((c, g, p), f32),
                   jax.ShapeDtypeStruct((c, g, p), f32),
                   jax.ShapeDtypeStruct((N_POW, g, p), f32),
                   jax.ShapeDtypeStruct((N_POW, g, p), f32)),
        name="ssm_params",
    )(lam_re.astype(f32), lam_im.astype(f32), log_dt.astype(f32).reshape(g, 1),
      jnp.transpose(b_re.astype(f32), (2, 0, 1)), jnp.transpose(b_im.astype(f32), (2, 0, 1)))


def _in_proj_kernel(x_ref, meta_ref, w_ref, o_ref, xb_ref):
    j = pl.program_id(1)
    n = pl.program_id(2)

    col = n * PROJ_TN + lax.broadcasted_iota(jnp.int32, (1, PROJ_TN), 1)
    is_q = jnp.logical_and(col >= Q_COL0, col < Q_COL0 + SB_WIDTH)
    scale = jnp.where(is_q, LOG2_E / math.sqrt(SB_HEAD_DIM), 1.0).astype(jnp.float32)

    def project(rows_bf):
        acc = jnp.dot(rows_bf, w_ref[...], preferred_element_type=jnp.float32)
        return (acc * scale).astype(o_ref.dtype)

    @pl.when(j == 0)
    def _():
        pad = PROJ_TM - META_ROWS
        meta_rows = jnp.concatenate([jnp.zeros((META_ROWS - N_META, D_MODEL), jnp.bfloat16),
                                     meta_ref[...].astype(jnp.bfloat16)], axis=0)
        res = project(meta_rows)
        for c in range(PROJ_TN // SLAB):
            o_ref[0, c, 0:pad, :] = jnp.zeros((pad, SLAB), o_ref.dtype)
            o_ref[0, c, pad:PROJ_TM, :] = res[:, c * SLAB:(c + 1) * SLAB]

    def store(res):
        for c in range(PROJ_TN // SLAB):
            o_ref[0, c] = res[:, c * SLAB:(c + 1) * SLAB]

    @pl.when(jnp.logical_and(j > 0, n == 0))
    def _():
        xb = x_ref[0].astype(jnp.bfloat16)
        xb_ref[...] = xb
        store(project(xb))

    @pl.when(jnp.logical_and(j > 0, n > 0))
    def _():
        store(project(xb_ref[...]))


def _in_proj(x, meta_tokens, w_in_bf):
    bsz, seq, _ = x.shape
    assert FRONT == PROJ_TM
    first = FRONT // PROJ_TM - 1
    n_tiles = seq // PROJ_TM + 1
    lp = seq + FRONT
    return pl.pallas_call(
        _in_proj_kernel,
        grid=(bsz, n_tiles, IN_WIDTH // PROJ_TN),
        in_specs=[
            pl.BlockSpec((1, PROJ_TM, D_MODEL), lambda b, j, n: (b, jnp.maximum(j - 1, 0), 0)),
            _resident((N_META, D_MODEL), lambda b, j, n: (0, 0)),
            pl.BlockSpec((D_MODEL, PROJ_TN), lambda b, j, n: (0, n)),
        ],
        out_specs=pl.BlockSpec((1, PROJ_TN // SLAB, PROJ_TM, SLAB),
                               lambda b, j, n: (b, n, first + j, 0)),
        out_shape=jax.ShapeDtypeStruct((bsz, IN_WIDTH // SLAB, lp, SLAB), jnp.bfloat16),
        scratch_shapes=[pltpu.VMEM((PROJ_TM, D_MODEL), jnp.bfloat16)],
        compiler_params=pltpu.CompilerParams(
            dimension_semantics=("arbitrary", "arbitrary", "arbitrary"),
            vmem_limit_bytes=VMEM_LIMIT),
        name="in_proj",
    )(x, meta_tokens, w_in_bf)


def _gelu_tanh(x):
    return 0.5 * x * (1.0 + jnp.tanh(math.sqrt(2.0 / math.pi) * (x + 0.044715 * (x * x * x))))


def _sigmoid(x):
    return 1.0 / (1.0 + jnp.exp(-x))


def _ssm_kernel(un_ref, g_ref, perm_ref, permt_ref, bw_ref, cw_ref, a1r_ref, a1i_ref,
                sgr_ref, sgi_ref, sfr_ref, sfi_ref, fr_ref, fi_ref,
                d_ref, wglu_ref, bglu_ref, o_ref, sa_ref, sb_ref, ua_ref, ub_ref, carry_ref):
    s = pl.program_id(1)
    f32 = jnp.float32
    bf16 = jnp.bfloat16

    @pl.when(s == 0)
    def _():
        carry_ref[...] = jnp.zeros_like(carry_ref)
        sb_ref[...] = jnp.zeros_like(sb_ref)
        ub_ref[...] = jnp.zeros_like(ub_ref)

    def step(sw_ref, sr_ref, uw_ref, ur_ref):
        perm = perm_ref[...]

        def permuted(ref):
            return jnp.concatenate([jnp.dot(perm, ref[0, c], preferred_element_type=f32)
                                    for c in range(ref.shape[1])], axis=1)

        un_p = permuted(un_ref).astype(bf16)
        uw_ref[...] = un_p
        u_p = ur_ref[...]
        g_p = permuted(g_ref)

        row = lax.broadcasted_iota(jnp.int32, (SUBLANES, CHUNK_ST), 0)
        ys = []
        for kc in range(SSM_CHUNKS):
            sw_ref[:, kc * 2 * CHUNK_ST:(kc + 1) * 2 * CHUNK_ST] = jnp.dot(
                un_p[:, kc * CHUNK_CH:(kc + 1) * CHUNK_CH], bw_ref[kc], preferred_element_type=f32)
            re0 = kc * 2 * CHUNK_ST
            im0 = re0 + CHUNK_ST
            cm0 = kc * CHUNK_ST
            last = SSM_TM - SUBLANES
            er = sr_ref[last:SSM_TM, re0:re0 + CHUNK_ST]
            ei = sr_ref[last:SSM_TM, im0:im0 + CHUNK_ST]
            for k, d in enumerate(SEG_STEPS):
                mr = sgr_ref[k, :, cm0:cm0 + CHUNK_ST]
                mi = sgi_ref[k, :, cm0:cm0 + CHUNK_ST]
                pr = pltpu.roll(er, d, 0)
                pi = pltpu.roll(ei, d, 0)
                er, ei = er + (mr * pr - mi * pi), ei + (mr * pi + mi * pr)
            c_r = carry_ref[:, re0:re0 + CHUNK_ST]
            c_i = carry_ref[:, im0:im0 + CHUNK_ST]
            sfr = sfr_ref[:, cm0:cm0 + CHUNK_ST]
            sfi = sfi_ref[:, cm0:cm0 + CHUNK_ST]
            hr = er + (sfr * c_r - sfi * c_i)
            hi = ei + (sfr * c_i + sfi * c_r)
            in_r = jnp.where(row == 0, c_r, pltpu.roll(hr, 1, 0))
            in_i = jnp.where(row == 0, c_i, pltpu.roll(hi, 1, 0))
            carry_ref[:, re0:re0 + CHUNK_ST] = jnp.broadcast_to(hr[SUBLANES - 1:SUBLANES, :],
                                                                 (SUBLANES, CHUNK_ST))
            carry_ref[:, im0:im0 + CHUNK_ST] = jnp.broadcast_to(hi[SUBLANES - 1:SUBLANES, :],
                                                                 (SUBLANES, CHUNK_ST))
            tin_r = jnp.concatenate([in_r] * SEG_LEN, axis=0)
            tin_i = jnp.concatenate([in_i] * SEG_LEN, axis=0)
            fr = fr_ref[:, cm0:cm0 + CHUNK_ST]
            fi = fi_ref[:, cm0:cm0 + CHUNK_ST]
            h_r = sr_ref[:, re0:re0 + CHUNK_ST] + (fr * tin_r - fi * tin_i)
            h_i = sr_ref[:, im0:im0 + CHUNK_ST] + (fr * tin_i + fi * tin_r)
            h_bf = jnp.concatenate([h_r.astype(bf16), h_i.astype(bf16)], axis=1)
            ys.append(jnp.dot(h_bf, cw_ref[kc], preferred_element_type=f32))
        y = jnp.concatenate(ys, axis=1) + d_ref[...] * u_p.astype(f32)
        z = _gelu_tanh(y).astype(bf16)
        hg = jnp.dot(z, wglu_ref[...], preferred_element_type=f32) + bglu_ref[...]
        glu = hg[:, :SSM_WIDTH] * _sigmoid(hg[:, SSM_WIDTH:])
        out_p = (glu * (g_p * _sigmoid(g_p))).astype(bf16)
        o_ref[0] = jnp.dot(permt_ref[...], out_p, preferred_element_type=f32).astype(o_ref.dtype)

        lane_tiles = CHUNK_ST // LANES
        for kc in range(SSM_CHUNKS):
            cols = [(kc * 2 * CHUNK_ST + c * LANES,
                     kc * 2 * CHUNK_ST + CHUNK_ST + c * LANES,
                     kc * CHUNK_ST + c * LANES) for c in range(lane_tiles)]

            def body(t, state, cols=cols):
                r0 = pl.multiple_of(t * SUBLANES, SUBLANES)
                new = []
                for idx, (cr, ci, cm) in enumerate(cols):
                    ar = a1r_ref[:, cm:cm + LANES]
                    ai = a1i_ref[:, cm:cm + LANES]
                    xr = state[2 * idx]
                    xi = state[2 * idx + 1]
                    nr = (ar * xr - ai * xi) + sw_ref[pl.ds(r0, SUBLANES), cr:cr + LANES]
                    ni = (ar * xi + ai * xr) + sw_ref[pl.ds(r0, SUBLANES), ci:ci + LANES]
                    sw_ref[pl.ds(r0, SUBLANES), cr:cr + LANES] = nr
                    sw_ref[pl.ds(r0, SUBLANES), ci:ci + LANES] = ni
                    new.append(nr)
                    new.append(ni)
                return tuple(new)

            zero = jnp.zeros((SUBLANES, LANES), f32)
            lax.fori_loop(0, SEG_LEN, body, tuple(zero for _ in range(2 * lane_tiles)), unroll=8)

    @pl.when(s % 2 == 0)
    def _():
        step(sa_ref, sb_ref, ua_ref, ub_ref)

    @pl.when(s % 2 == 1)
    def _():
        step(sb_ref, sa_ref, ub_ref, ua_ref)


def _ssm(proj, perm, perm_t, bw, cw, tables, d_row, w_glu_bf, b_glu_row, seq):
    bsz = proj.shape[0]
    t0 = FRONT // SSM_TM - 1
    n_tiles = seq // SSM_TM + 1
    consts = (perm, perm_t, bw, cw) + tuple(tables) + (d_row, w_glu_bf, b_glu_row)

    def const_spec(a):
        nd = a.ndim
        return _resident(a.shape, lambda b, s, nd=nd: (0,) * nd)

    def cur(s):
        return jnp.maximum(s - 1, 0)

    ns = SSM_WIDTH // SLAB

    return pl.pallas_call(
        _ssm_kernel,
        grid=(bsz, n_tiles + 1),
        in_specs=[
            pl.BlockSpec((1, ns, SSM_TM, SLAB), lambda b, s: (b, 0, t0 + jnp.minimum(s, n_tiles - 1), 0)),
            pl.BlockSpec((1, ns, SSM_TM, SLAB), lambda b, s: (b, 1, t0 + cur(s), 0)),
        ] + [const_spec(a) for a in consts],
        out_specs=pl.BlockSpec((1, SSM_TM, SSM_WIDTH), lambda b, s: (b, jnp.maximum(s - 2, 0), 0)),
        out_shape=jax.ShapeDtypeStruct((bsz, seq, SSM_WIDTH), jnp.bfloat16),
        scratch_shapes=[pltpu.VMEM((SSM_TM, 2 * N_STATE), jnp.float32),
                        pltpu.VMEM((SSM_TM, 2 * N_STATE), jnp.float32),
                        pltpu.VMEM((SSM_TM, SSM_WIDTH), jnp.bfloat16),
                        pltpu.VMEM((SSM_TM, SSM_WIDTH), jnp.bfloat16),
                        pltpu.VMEM((SUBLANES, 2 * N_STATE), jnp.float32)],
        compiler_params=pltpu.CompilerParams(
            dimension_semantics=("arbitrary", "arbitrary"),
            vmem_limit_bytes=VMEM_LIMIT),
        name="ssm",
    )(proj, proj, *consts)


def _attention_kernel(*refs):
    q_refs = refs[:ATT_QPARTS]
    k_ref, v_ref = refs[ATT_QPARTS:ATT_QPARTS + 2]
    g_refs = refs[ATT_QPARTS + 2:2 * ATT_QPARTS + 2]
    tri_ref, o_ref, acc_ref, run_ref = refs[2 * ATT_QPARTS + 2:]
    qi = pl.program_id(2)
    t = ATT_T
    hd = SB_HEAD_DIM
    f32 = jnp.float32
    bf16 = jnp.bfloat16
    qs = [FRONT + (qi * ATT_QBLOCKS + c) * t for c in range(ATT_QBLOCKS)]
    n_blocks = (qi + 1) * ATT_QBLOCKS + 1
    row = lax.broadcasted_iota(jnp.int32, (t, t), 0)
    col = lax.broadcasted_iota(jnp.int32, (t, t), 1)

    def block(c, h, ks, run, causal):
        hs = slice(h * hd, (h + 1) * hd)
        r0 = c * t % ATT_QPART
        q = q_refs[c * t // ATT_QPART][0, 0, r0:r0 + t, hs]
        kb = k_ref[0, 0, pl.ds(ks, t), hs]
        vb = v_ref[0, 0, pl.ds(ks, t), hs]
        z = lax.dot_general(q, kb, (((1,), (1,)), ((), ())), preferred_element_type=f32)
        if causal:
            z = jnp.where(col < row, z, MASKED_SCORE)
        sp = jnp.maximum(z, 0.0) + jnp.log2(1.0 + jnp.exp2(-jnp.abs(z)))
        hi = sp.astype(bf16)
        lo = (sp - hi.astype(f32)).astype(bf16)
        nsum = jnp.dot(jnp.concatenate([hi, lo], axis=1), tri_ref[...],
                       preferred_element_type=f32)
        a = jnp.exp2(z + nsum + jnp.concatenate([run] * (t // LANES), axis=1))
        pv = jnp.dot(a.astype(bf16), vb, preferred_element_type=f32)
        return pv, run + jnp.broadcast_to(nsum[:, 0:1], (t, LANES))

    def pair(c, h, ks, run, first):
        pv0, run = block(c, h, pl.multiple_of(jnp.maximum(ks, FIRST_ROW), t), run, first)
        pv1, run = block(c, h, pl.multiple_of(jnp.maximum(ks - t, FIRST_ROW), t), run, False)
        return pv0 + pv1, run

    live = jnp.bool_(False)
    for c in range(ATT_QBLOCKS):
        rows = slice(c * t, (c + 1) * t)
        for h in range(ATT_HEADS):
            lanes = slice(h * LANES, (h + 1) * LANES)
            pv, run = pair(c, h, qs[c], jnp.zeros((t, LANES), f32), True)
            acc_ref[rows, lanes] = pv
            run_ref[rows, lanes] = run
            live = jnp.logical_or(live, jnp.max(run) >= EXP2_ZERO_BELOW)

    def cond(state):
        j, live = state
        return jnp.logical_and(j < n_blocks, live)

    def body(state):
        j, _ = state
        live = jnp.bool_(False)
        for c in range(ATT_QBLOCKS):
            rows = slice(c * t, (c + 1) * t)
            for h in range(ATT_HEADS):
                lanes = slice(h * LANES, (h + 1) * LANES)
                pv, run = pair(c, h, qs[c] - j * t, run_ref[rows, lanes], False)
                acc_ref[rows, lanes] += pv
                run_ref[rows, lanes] = run
                live = jnp.logical_or(live, jnp.max(run) >= EXP2_ZERO_BELOW)
        return j + 2, live

    lax.while_loop(cond, body, (jnp.int32(2), live))
    for p in range(ATT_QPARTS):
        rows = slice(p * ATT_QPART, (p + 1) * ATT_QPART)
        g = g_refs[p][0, 0].astype(f32)
        o_ref[0, rows] = (acc_ref[rows] * (g * _sigmoid(g))).astype(o_ref.dtype)


def _attention(proj, tri, seq):
    bsz, _, lp, w = proj.shape
    t = ATT_T
    q_col0 = 2 * SSM_WIDTH // w
    k_col0 = q_col0 + SB_WIDTH // w
    v_col0 = k_col0 + SB_WIDTH // w
    g_col0 = v_col0 + SB_WIDTH // w
    tq = ATT_QBLOCKS * t
    assert FRONT % ATT_QPART == 0 and FRONT >= 2 * t

    def part_spec(col0, p):
        return pl.BlockSpec((1, 1, ATT_QPART, w),
                            lambda b, h, i: (b, col0 + h, FRONT // ATT_QPART + i * ATT_QPARTS + p, 0))

    return pl.pallas_call(
        _attention_kernel,
        grid=(bsz, SB_HEADS // ATT_HEADS, seq // tq),
        in_specs=[part_spec(q_col0, p) for p in range(ATT_QPARTS)] + [
            pl.BlockSpec((1, 1, lp, w), lambda b, h, i: (b, k_col0 + h, 0, 0)),
            pl.BlockSpec((1, 1, lp, w), lambda b, h, i: (b, v_col0 + h, 0, 0)),
        ] + [part_spec(g_col0, p) for p in range(ATT_QPARTS)] + [
            _resident(tri.shape, lambda b, h, i: (0, 0)),
        ],
        out_specs=pl.BlockSpec((1, tq, w), lambda b, h, i: (b, i, h)),
        out_shape=jax.ShapeDtypeStruct((bsz, seq, SB_WIDTH), jnp.bfloat16),
        scratch_shapes=[pltpu.VMEM((tq, w), jnp.float32),
                        pltpu.VMEM((tq, ATT_HEADS * LANES), jnp.float32)],
        compiler_params=pltpu.CompilerParams(
            dimension_semantics=("arbitrary", "arbitrary", "arbitrary"),
            vmem_limit_bytes=VMEM_LIMIT),
        name="attention",
    )(*([proj] * (2 * ATT_QPARTS + 2)), tri)


def _mix_out_kernel(x_ref, a_ref, s_ref, wa_ref, ws_ref, wg_ref, bg_ref,
                    wo_ref, gain_ref, bias_ref, o_ref):
    f32 = jnp.float32
    x = x_ref[0]
    xb = x.astype(jnp.bfloat16)
    a = a_ref[0]
    s = s_ref[0]
    y_ssm = jnp.dot(a, wa_ref[...], preferred_element_type=f32)
    y_sb = jnp.dot(s, ws_ref[...], preferred_element_type=f32)
    gates = _sigmoid(jnp.dot(xb, wg_ref[...], preferred_element_type=f32) + bg_ref[...])
    mixed = (gates[:, :D_MODEL] * y_ssm + gates[:, D_MODEL:] * y_sb).astype(jnp.bfloat16)
    r = DEEPNORM_ALPHA * x + jnp.dot(mixed, wo_ref[...], preferred_element_type=f32)
    mu = jnp.mean(r, axis=-1, keepdims=True)
    c = r - mu
    var = jnp.mean(c * c, axis=-1, keepdims=True)
    o_ref[0] = c * lax.rsqrt(var + LN_EPS) * gain_ref[...] + bias_ref[...]


def _mix_out(x, a_ssm, a_sb, wa, ws, wg, bg_row, wo, gain_row, bias_row):
    bsz, seq, _ = x.shape
    consts = (wa, ws, wg, bg_row, wo, gain_row, bias_row)
    return pl.pallas_call(
        _mix_out_kernel,
        grid=(bsz, seq // MIX_TM),
        in_specs=[
            pl.BlockSpec((1, MIX_TM, D_MODEL), lambda b, j: (b, j, 0)),
            pl.BlockSpec((1, MIX_TM, SSM_WIDTH), lambda b, j: (b, j, 0)),
            pl.BlockSpec((1, MIX_TM, SB_WIDTH), lambda b, j: (b, j, 0)),
        ] + [_resident(w.shape, lambda b, j: (0, 0)) for w in consts],
        out_specs=pl.BlockSpec((1, MIX_TM, D_MODEL), lambda b, j: (b, j, 0)),
        out_shape=jax.ShapeDtypeStruct((bsz, seq, D_MODEL), jnp.float32),
        compiler_params=pltpu.CompilerParams(
            dimension_semantics=("arbitrary", "arbitrary"),
            vmem_limit_bytes=VMEM_LIMIT),
        name="mix_out",
    )(x, a_ssm, a_sb, *consts)


def _block_diag_b(bb_r, bb_i):
    gl = SSM_GROUPS // SSM_CHUNKS
    eye = jnp.eye(gl, dtype=jnp.float32)

    def part(bb):
        b4 = jnp.transpose(bb, (1, 0, 2)).reshape(SSM_CHUNKS, gl, SSM_GROUP, SSM_STATE)
        return jnp.einsum('kgcp,gh->kgchp', b4, eye).reshape(SSM_CHUNKS, CHUNK_CH, CHUNK_ST)

    return jnp.concatenate([part(bb_r), part(bb_i)], axis=2).astype(jnp.bfloat16)


def _block_diag_c(c_re, c_im):
    gl = SSM_GROUPS // SSM_CHUNKS
    eye = jnp.eye(gl, dtype=jnp.float32)

    def part(c):
        c4 = c.astype(jnp.float32).reshape(SSM_CHUNKS, gl, SSM_GROUP, SSM_STATE)
        return jnp.einsum('kgcp,gh->kgphc', c4, eye).reshape(SSM_CHUNKS, CHUNK_ST, CHUNK_CH)

    return jnp.concatenate([part(c_re), -part(c_im)], axis=1).astype(jnp.bfloat16)


def _scan_tables(pw_r, pw_i):
    rows = jnp.arange(SUBLANES)[:, None]

    def tables(pw):
        p = pw.reshape(N_POW, N_STATE)
        a1 = jnp.broadcast_to(p[0][None, :], (SUBLANES, N_STATE))
        seg = p[SEG_LEN - 1:]
        sg = jnp.stack([jnp.where(rows >= d, seg[d - 1][None, :], 0.0) for d in SEG_STEPS])
        f = jnp.repeat(p[:SEG_LEN], SUBLANES, axis=0)
        return a1, sg, seg, f

    a1r, sgr, sfr, fr = tables(pw_r)
    a1i, sgi, sfi, fi = tables(pw_i)
    return a1r, a1i, sgr, sgi, sfr, sfi, fr, fi


def _scan_permutation():
    r = np.arange(SSM_TM)
    tok = (r % SUBLANES) * SEG_LEN + r // SUBLANES
    perm = (tok[:, None] == r[None, :]).astype(np.float32)
    return jnp.asarray(perm, jnp.bfloat16), jnp.asarray(perm.T, jnp.bfloat16)


def kernel(x, meta_tokens, w_in, ssm_lambda_re, ssm_lambda_im, ssm_log_dt, ssm_b_re, ssm_b_im,
           ssm_c_re, ssm_c_im, ssm_d, w_glu, b_glu, w_branch_ssm, w_branch_sb, w_gate, b_gate,
           w_out, ln_gain, ln_bias):
    bsz, seq, _ = x.shape
    f32 = jnp.float32
    bf16 = jnp.bfloat16
    layer = 0

    bb_r, bb_i, pw_r, pw_i = _ssm_params(ssm_lambda_re[layer], ssm_lambda_im[layer],
                                         ssm_log_dt[layer], ssm_b_re[layer], ssm_b_im[layer])
    bw = _block_diag_b(bb_r, bb_i)
    cw = _block_diag_c(ssm_c_re[layer], ssm_c_im[layer])
    tables = _scan_tables(pw_r, pw_i)
    perm, perm_t = _scan_permutation()

    proj = _in_proj(x.astype(f32), meta_tokens.astype(f32), w_in[layer].astype(bf16))

    a_ssm = _ssm(proj, perm, perm_t, bw, cw, tables,
                 ssm_d[layer].astype(f32).reshape(1, SSM_WIDTH),
                 w_glu[layer].astype(bf16), b_glu[layer].astype(f32).reshape(1, 2 * SSM_WIDTH), seq)

    t = ATT_T
    r = np.arange(t)
    neg_lower = -(r[:, None] >= r[None, :]).astype(np.float32)
    tri = jnp.asarray(np.concatenate([neg_lower, neg_lower], axis=0), bf16)
    a_sb = _attention(proj, tri, seq)

    return _mix_out(x.astype(f32), a_ssm, a_sb,
                    w_branch_ssm[layer].astype(bf16), w_branch_sb[layer].astype(bf16),
                    w_gate[layer].astype(bf16), b_gate[layer].astype(f32).reshape(1, 2 * D_MODEL),
                    w_out[layer].astype(bf16),
                    ln_gain[layer].astype(f32).reshape(1, D_MODEL),
                    ln_bias[layer].astype(f32).reshape(1, D_MODEL))
```
